```python
import math
import jax, jax.numpy as jnp
from jax import lax
import numpy as np

D_MODEL = 2048
BATCH = 4
SEQ = 4096
DEPTH = 2

MIX = D_MODEL
D_MLA = MIX // 2
D_CONV = MIX - D_MLA
N_HEADS = 8
NOPE_DIM = 128
ROPE_DIM = 64
V_DIM = D_MLA // N_HEADS
QK_DIM = NOPE_DIM + ROPE_DIM
Q_LORA = 512
KV_LORA = 256
ROPE_THETA = 10000.0
Q_BLOCK = 128
CONV_K = 31
IN_COLS = Q_LORA + KV_LORA + ROPE_DIM + D_MLA + 2 * D_CONV + D_CONV
EPS = 1e-6

kernel_name = "hybrid_mla_conformer_conv_headgroups"


def rms_norm(x, g):
    xf = x.astype(jnp.float32)
    y = xf * lax.rsqrt(jnp.mean(xf * xf, axis=-1, keepdims=True) + EPS)
    return (y * g.astype(jnp.float32)).astype(x.dtype)


def layer_norm(x, g, b):
    xf = x.astype(jnp.float32)
    mu = jnp.mean(xf, axis=-1, keepdims=True)
    var = jnp.mean(jnp.square(xf - mu), axis=-1, keepdims=True)
    y = (xf - mu) * lax.rsqrt(var + EPS)
    return (y * g.astype(jnp.float32) + b.astype(jnp.float32)).astype(x.dtype)


def rope_tables(positions, dtype):
    inv_freq = 1.0 / (ROPE_THETA ** (jnp.arange(0, ROPE_DIM, 2, dtype=jnp.float32) / ROPE_DIM))
    ang = positions.astype(jnp.float32)[..., None] * inv_freq
    return jnp.cos(ang)[:, :, None, :].astype(dtype), jnp.sin(ang)[:, :, None, :].astype(dtype)


def apply_rope(x, cos, sin):
    x1, x2 = jnp.split(x, 2, axis=-1)
    return jnp.concatenate([x1 * cos - x2 * sin, x2 * cos + x1 * sin], axis=-1)


def causal_block_attention(q, k, v):
    b, s, h, dq = q.shape
    nb = s // Q_BLOCK
    scale = 1.0 / math.sqrt(dq)
    qb = q.reshape(b, nb, Q_BLOCK, h, dq).transpose(1, 0, 2, 3, 4)
    key_pos = jnp.arange(s)

    def one_block(args):
        q_i, blk = args
        scores = jnp.einsum('bqhd,bkhd->bhqk', q_i, k).astype(jnp.float32) * scale
        q_pos = blk * Q_BLOCK + jnp.arange(Q_BLOCK)
        mask = key_pos[None, :] <= q_pos[:, None]
        scores = jnp.where(mask[None, None], scores, -jnp.inf)
        p = jax.nn.softmax(scores, axis=-1).astype(v.dtype)
        return jnp.einsum('bhqk,bkhd->bqhd', p, v)

    out = lax.map(one_block, (qb, jnp.arange(nb)))
    return out.transpose(1, 0, 2, 3, 4).reshape(b, s, h, v.shape[-1])


def causal_depthwise_conv(u, w, bias):
    out = lax.conv_general_dilated(
        u, w[:, None, :].astype(u.dtype),
        window_strides=(1,), padding=((CONV_K - 1, 0),),
        dimension_numbers=('NWC', 'WIO', 'NWC'),
        feature_group_count=u.shape[-1])
    return out + bias


def setup_inputs(seed: int = 0) -> dict:
    key = jax.random.key(seed)
    ks = jax.random.split(key, 24)
    f32 = jnp.float32

    def w(k, shape, fan_in):
        return jax.random.normal(k, shape, f32) * (fan_in ** -0.5)

    def gain(k, shape):
        return 1.0 + 0.05 * jax.random.normal(k, shape, f32)

    def small(k, shape):
        return 0.02 * jax.random.normal(k, shape, f32)

    x = jax.random.normal(ks[0], (BATCH, SEQ, D_MODEL), f32)
    c = jax.random.normal(ks[1], (BATCH, D_MODEL), f32)
    offsets = jax.random.randint(ks[2], (BATCH, 1), 0, 1024, dtype=jnp.int32)
    positions = offsets + jnp.arange(SEQ, dtype=jnp.int32)[None, :]
    return {
        'x': x,
        'c': c,
        'positions': positions,
        'ada_w': w(ks[3], (DEPTH, D_MODEL, 3 * D_MODEL), D_MODEL),
        'ada_b': small(ks[4], (DEPTH, 3 * D_MODEL)),
        'norm_g': gain(ks[5], (DEPTH, D_MODEL)),
        'w_in': w(ks[6], (DEPTH, D_MODEL, IN_COLS), D_MODEL),
        'q_lat_g': gain(ks[7], (DEPTH, Q_LORA)),
        'w_q_up': w(ks[8], (DEPTH, Q_LORA, N_HEADS * QK_DIM), Q_LORA),
        'kv_lat_g': gain(ks[9], (DEPTH, KV_LORA)),
        'w_kv_up': w(ks[10], (DEPTH, KV_LORA, N_HEADS * (NOPE_DIM + V_DIM)), KV_LORA),
        'q_norm_g': gain(ks[11], (DEPTH, QK_DIM)),
        'k_norm_g': gain(ks[12], (DEPTH, QK_DIM)),
        'glu_b': small(ks[13], (DEPTH, 2 * D_CONV)),
        'dw_w': w(ks[14], (DEPTH, CONV_K, D_CONV), CONV_K),
        'dw_b': small(ks[15], (DEPTH, D_CONV)),
        'conv_ln_g': gain(ks[16], (DEPTH, D_CONV)),
        'conv_ln_b': small(ks[17], (DEPTH, D_CONV)),
        'w_pw': w(ks[18], (DEPTH, D_CONV, D_CONV), D_CONV),
        'b_pw': small(ks[19], (DEPTH, D_CONV)),
        'w_out': w(ks[20], (DEPTH, MIX, D_MODEL), MIX),
    }


def reference(x, c, positions, ada_w, ada_b, norm_g, w_in, q_lat_g, w_q_up, kv_lat_g,
              w_kv_up, q_norm_g, k_norm_g, glu_b, dw_w, dw_b, conv_ln_g, conv_ln_b,
              w_pw, b_pw, w_out):
    b, s, _ = x.shape
    cos, sin = rope_tables(positions, x.dtype)
    c_act = jax.nn.silu(c)
    splits = np.cumsum([Q_LORA, KV_LORA, ROPE_DIM, D_MLA, 2 * D_CONV]).tolist()

    for l in range(DEPTH):
        mod = c_act @ ada_w[l] + ada_b[l]
        shift, scale, gate = [m[:, None, :] for m in jnp.split(mod, 3, axis=-1)]
        h = rms_norm(x, norm_g[l]) * (1.0 + scale) + shift

        z = h @ w_in[l]
        q_lat, kv_lat, k_rope, mla_gate, conv_in, conv_gate = jnp.split(z, splits, axis=-1)

        q = (rms_norm(q_lat, q_lat_g[l]) @ w_q_up[l]).reshape(b, s, N_HEADS, QK_DIM)
        kv = (rms_norm(kv_lat, kv_lat_g[l]) @ w_kv_up[l]).reshape(b, s, N_HEADS, NOPE_DIM + V_DIM)
        k_nope, v = kv[..., :NOPE_DIM], kv[..., NOPE_DIM:]
        k_rope_h = jnp.broadcast_to(k_rope[:, :, None, :], (b, s, N_HEADS, ROPE_DIM))
        k = jnp.concatenate([k_nope, k_rope_h], axis=-1)
        q = rms_norm(q, q_norm_g[l])
        k = rms_norm(k, k_norm_g[l])
        q = jnp.concatenate([q[..., :NOPE_DIM], apply_rope(q[..., NOPE_DIM:], cos, sin)], axis=-1)
        k = jnp.concatenate([k[..., :NOPE_DIM], apply_rope(k[..., NOPE_DIM:], cos, sin)], axis=-1)
        attn = causal_block_attention(q, k, v).reshape(b, s, D_MLA)
        mla_out = attn * jax.nn.silu(mla_gate)

        u_val, u_gate = jnp.split(conv_in + glu_b[l], 2, axis=-1)
        u = u_val * jax.nn.sigmoid(u_gate)
        u = causal_depthwise_conv(u, dw_w[l], dw_b[l])
        u = jax.nn.silu(layer_norm(u, conv_ln_g[l], conv_ln_b[l]))
        u = u @ w_pw[l] + b_pw[l]
        conv_out = u * jax.nn.silu(conv_gate)

        y = jnp.concatenate([mla_out, conv_out], axis=-1) @ w_out[l]
        x = x + gate * y
    return x
```

```python
import functools
import math

import jax
import jax.numpy as jnp
from jax import lax
from jax.experimental import pallas as pl
from jax.experimental.pallas import tpu as pltpu

F32 = jnp.float32
BF16 = jnp.bfloat16

N_HEADS = 8
NOPE_DIM = 128
ROPE_DIM = 64
V_DIM = 128
QK_DIM = NOPE_DIM + ROPE_DIM
Q_LORA = 512
KV_LORA = 256
ROPE_THETA = 10000.0
CONV_K = 31
EPS = 1e-6
LANES = 128

Z_CONV_IN = 0
Z_MLA_GATE = 2048
Z_CONV_GATE = 3072
Z_Q_LAT = 4096
Z_KV_LAT = 4608
Z_K_ROPE = 4864
Z_COLS = 5120

VMEM_LIMIT = 56 * 1024 * 1024


def _sigmoid(x):
    return 1.0 / (1.0 + jnp.exp(-x))


def _silu(x):
    return x * _sigmoid(x)


def _params(sem, vmem=VMEM_LIMIT):
    return pltpu.CompilerParams(dimension_semantics=sem, vmem_limit_bytes=vmem)


def _rope_table_kernel(pos_ref, inv_ref, sign_ref, cos_ref, sin_ref):
    ang = pos_ref[...].astype(F32) * inv_ref[...]
    cos_ref[...] = jnp.cos(ang)
    sin_ref[...] = jnp.sin(ang) * sign_ref[...]


def _rope_tables(pos_b, inv128, sign128, tr):
    t = pos_b.shape[0]
    row = pl.BlockSpec((tr, LANES), lambda i: (i, 0))
    vec = pl.BlockSpec((1, LANES), lambda i: (0, 0))
    return pl.pallas_call(
        _rope_table_kernel,
        grid=(t // tr,),
        in_specs=[row, vec, vec],
        out_specs=[row, row],
        out_shape=[jax.ShapeDtypeStruct((t, LANES), F32)] * 2,
        compiler_params=_params(("parallel",)),
        name="rope_tables",
    )(pos_b, inv128, sign128)


def _adaln_kernel(c_ref, w_ref, b_ref, o_ref):
    c_act = _silu(c_ref[...]).astype(BF16)
    acc = jnp.dot(c_act, w_ref[0].astype(BF16), preferred_element_type=F32)
    o_ref[0] = acc + b_ref[0]


def _adaln(c_pad, ada_w, ada_b3, tn):
    depth, d, n = ada_w.shape
    rows = c_pad.shape[0]
    return pl.pallas_call(
        _adaln_kernel,
        grid=(depth, n // tn),
        in_specs=[
            pl.BlockSpec((rows, d), lambda l, j: (0, 0)),
            pl.BlockSpec((1, d, tn), lambda l, j: (l, 0, j)),
            pl.BlockSpec((1, 1, tn), lambda l, j: (l, 0, j)),
        ],
        out_specs=pl.BlockSpec((1, rows, tn), lambda l, j: (l, 0, j)),
        out_shape=jax.ShapeDtypeStruct((depth, rows, n), F32),
        compiler_params=_params(("parallel", "parallel")),
        name="adaln_mod",
    )(c_pad, ada_w, ada_b3)


def _inproj_kernel(x_ref, shift_ref, scale_ref, g_ref, w_ref, z_ref, h_ref, *, rc):
    tm, d = x_ref.shape

    @pl.when(pl.program_id(1) == 0)
    def _():
        mult = g_ref[...] * (1.0 + scale_ref[0])
        shift = shift_ref[0]

        def body(r, carry):
            rows = pl.ds(pl.multiple_of(r * rc, rc), rc)
            x = x_ref[rows, :]
            inv = lax.rsqrt(jnp.sum(x * x, axis=-1, keepdims=True) * (1.0 / d) + EPS)
            h_ref[rows, :] = (x * inv * mult + shift).astype(BF16)
            return carry

        lax.fori_loop(0, tm // rc, body, 0)

    z_ref[...] = jnp.dot(h_ref[...], w_ref[...], preferred_element_type=F32).astype(BF16)


def _inproj(x2, mod3, g, w, seq, tm, tn):
    t, d = x2.shape
    ncol = w.shape[1]
    tiles_per_seq = seq // tm

    def mod_spec(chunk):
        return pl.BlockSpec((1, 1, d), lambda i, j: (i // tiles_per_seq, 0, chunk))

    return pl.pallas_call(
        functools.partial(_inproj_kernel, rc=64),
        grid=(t // tm, ncol // tn),
        in_specs=[
            pl.BlockSpec((tm, d), lambda i, j: (i, 0)),
            mod_spec(0),
            mod_spec(1),
            pl.BlockSpec((1, d), lambda i, j: (0, 0)),
            pl.BlockSpec((d, tn), lambda i, j: (0, j)),
        ],
        out_specs=pl.BlockSpec((tm, tn), lambda i, j: (i, j)),
        out_shape=jax.ShapeDtypeStruct((t, ncol), BF16),
        scratch_shapes=[pltpu.VMEM((tm, d), BF16)],
        compiler_params=_params(("parallel", "arbitrary")),
        name="inproj",
    )(x2, mod3, mod3, g, w)


def _rms_rows(x, g):
    inv = lax.rsqrt(jnp.mean(x * x, axis=-1, keepdims=True) + EPS)
    return x * inv * g


def _rotate_half_pairs(x):
    lane = lax.broadcasted_iota(jnp.int32, x.shape, 1)
    first = (lane % ROPE_DIM) < (ROPE_DIM // 2)
    return jnp.where(first, pltpu.roll(x, LANES - ROPE_DIM // 2, 1), pltpu.roll(x, ROPE_DIM // 2, 1))


def _qkv_kernel(qlat_ref, kvlat_ref, krope_ref, cos_ref, sin_ref, gql_ref, gkvl_ref, wq_ref, wkv_ref,
                gqn_ref, gqr_ref, gkn_ref, gkr_ref, q_ref, k_ref, v_ref):
    cos = cos_ref[...]
    sin = sin_ref[...]
    lane = lax.broadcasted_iota(jnp.int32, cos.shape, 1)
    low_half = lane < ROPE_DIM

    qn = _rms_rows(qlat_ref[...].astype(F32), gql_ref[...]).astype(BF16)
    q = jnp.dot(qn, wq_ref[...], preferred_element_type=F32)
    kvn = _rms_rows(kvlat_ref[...].astype(F32), gkvl_ref[...]).astype(BF16)
    kv = jnp.dot(kvn, wkv_ref[...], preferred_element_type=F32)

    kr = krope_ref[...].astype(F32)
    kr_ss = jnp.sum(kr * kr, axis=-1, keepdims=True)
    krg = kr * gkr_ref[...]
    kr_rot = krg * cos + _rotate_half_pairs(krg) * sin

    q_scale = 1.0 / math.sqrt(QK_DIM)
    rope_base = N_HEADS * NOPE_DIM
    for pair in range(N_HEADS // 2):
        qr = q[:, rope_base + pair * LANES: rope_base + (pair + 1) * LANES]
        qr_sq = qr * qr
        qrg = qr * gqr_ref[...]
        qr_rot = qrg * cos + _rotate_half_pairs(qrg) * sin
        for sub in range(2):
            h = 2 * pair + sub
            q_nope = q[:, h * NOPE_DIM:(h + 1) * NOPE_DIM]
            mask = low_half if sub == 0 else jnp.logical_not(low_half)
            ss = (jnp.sum(q_nope * q_nope, axis=-1, keepdims=True)
                  + jnp.sum(jnp.where(mask, qr_sq, 0.0), axis=-1, keepdims=True))
            inv = lax.rsqrt(ss * (1.0 / QK_DIM) + EPS) * q_scale
            q_ref[0, h, :, 0:NOPE_DIM] = (q_nope * inv * gqn_ref[...]).astype(BF16)
            q_ref[0, h, :, NOPE_DIM:QK_DIM] = (
                qr_rot[:, sub * ROPE_DIM:(sub + 1) * ROPE_DIM] * inv).astype(BF16)

    for h in range(N_HEADS):
        k_nope = kv[:, h * 2 * NOPE_DIM: h * 2 * NOPE_DIM + NOPE_DIM]
        ss = jnp.sum(k_nope * k_nope, axis=-1, keepdims=True) + kr_ss
        inv = lax.rsqrt(ss * (1.0 / QK_DIM) + EPS)
        k_ref[0, h, :, 0:NOPE_DIM] = (k_nope * inv * gkn_ref[...]).astype(BF16)
        k_ref[0, h, :, NOPE_DIM:QK_DIM] = (kr_rot[:, 0:ROPE_DIM] * inv).astype(BF16)
        v_ref[0, h] = kv[:, h * 2 * NOPE_DIM + NOPE_DIM:(h + 1) * 2 * NOPE_DIM].astype(BF16)


def _qkv_up(z, cos_t, sin_t, gql, gkvl, wq, wkv, gqn, gqr, gkn, gkr, batch, seq, tm):
    tiles_per_seq = seq // tm

    def zcol(width, offset):
        return pl.BlockSpec((tm, width), lambda i: (i, offset // width))

    def full(arr):
        return pl.BlockSpec(arr.shape, lambda i: (0,) * arr.ndim)

    def head_out(width):
        return pl.BlockSpec((1, N_HEADS, tm, width),
                            lambda i: (i // tiles_per_seq, 0, i % tiles_per_seq, 0))

    row = pl.BlockSpec((tm, LANES), lambda i: (i, 0))
    return pl.pallas_call(
        _qkv_kernel,
        grid=(batch * tiles_per_seq,),
        in_specs=[zcol(Q_LORA, Z_Q_LAT), zcol(KV_LORA, Z_KV_LAT), zcol(LANES, Z_K_ROPE), row, row,
                  full(gql), full(gkvl), full(wq), full(wkv), full(gqn), full(gqr), full(gkn), full(gkr)],
        out_specs=[head_out(QK_DIM), head_out(QK_DIM), head_out(V_DIM)],
        out_shape=[jax.ShapeDtypeStruct((batch, N_HEADS, seq, QK_DIM), BF16),
                   jax.ShapeDtypeStruct((batch, N_HEADS, seq, QK_DIM), BF16),
                   jax.ShapeDtypeStruct((batch, N_HEADS, seq, V_DIM), BF16)],
        compiler_params=_params(("parallel",)),
        name="qkv_up",
    )(z, z, z, cos_t, sin_t, gql, gkvl, wq, wkv, gqn, gqr, gkn, gkr)


def _attn_kernel(q_ref, k_ref, v_ref, gate_ref, o_ref, m_ref, l_ref, acc_ref, *, tq, tk):
    qi = pl.program_id(2)
    q = q_ref[0, 0]

    m_ref[...] = jnp.full(m_ref.shape, -jnp.inf, F32)
    l_ref[...] = jnp.zeros(l_ref.shape, F32)
    acc_ref[...] = jnp.zeros(acc_ref.shape, F32)

    def step(kb, masked):
        rows = pl.ds(pl.multiple_of(kb * tk, tk), tk)
        k = k_ref[0, 0, rows, :]
        v = v_ref[0, 0, rows, :]
        s = lax.dot_general(q, k, (((1,), (1,)), ((), ())), preferred_element_type=F32)
        if masked:
            q_pos = qi * tq + lax.broadcasted_iota(jnp.int32, s.shape, 0)
            k_pos = kb * tk + lax.broadcasted_iota(jnp.int32, s.shape, 1)
            s = jnp.where(k_pos <= q_pos, s, -jnp.inf)
        m_prev = m_ref[...]
        m_new = jnp.maximum(m_prev, jnp.max(s, axis=-1, keepdims=True))
        alpha = jnp.exp(m_prev - m_new)
        p = jnp.exp(s - m_new)
        l_ref[...] = alpha * l_ref[...] + jnp.sum(p, axis=-1, keepdims=True)
        acc_ref[...] = alpha * acc_ref[...] + jnp.dot(p.astype(BF16), v, preferred_element_type=F32)
        m_ref[...] = m_new

    n_full = (qi * tq) // tk
    lax.fori_loop(0, n_full, lambda kb, c: (step(kb, False), c)[1], 0)
    for d in range(tq // tk):
        step(n_full + d, True)

    gate = gate_ref[...].astype(F32)
    o_ref[...] = (acc_ref[...] / l_ref[...] * _silu(gate)).astype(BF16)


def _attention(q, k, v, z, tq, tk):
    batch, heads, seq, _ = q.shape
    nq = seq // tq
    gate_col0 = Z_MLA_GATE // V_DIM
    return pl.pallas_call(
        functools.partial(_attn_kernel, tq=tq, tk=tk),
        grid=(batch, heads, nq),
        in_specs=[
            pl.BlockSpec((1, 1, tq, QK_DIM), lambda b, h, i: (b, h, i, 0)),
            pl.BlockSpec((1, 1, seq, QK_DIM), lambda b, h, i: (b, h, 0, 0)),
            pl.BlockSpec((1, 1, seq, V_DIM), lambda b, h, i: (b, h, 0, 0)),
            pl.BlockSpec((tq, V_DIM), lambda b, h, i: (b * nq + i, gate_col0 + h)),
        ],
        out_specs=pl.BlockSpec((tq, V_DIM), lambda b, h, i: (b * nq + i, h)),
        out_shape=jax.ShapeDtypeStruct((batch * seq, heads * V_DIM), BF16),
        scratch_shapes=[pltpu.VMEM((tq, 1), F32), pltpu.VMEM((tq, 1), F32), pltpu.VMEM((tq, V_DIM), F32)],
        compiler_params=_params(("parallel", "parallel", "arbitrary")),
        name="mla_attention",
    )(q, k, v, z)


HALO = 32


def _conv_kernel(uval_ref, ugate_ref, cgate_ref, glub_ref, dww_ref, dwb_ref, lng_ref, lnb_ref, wpw_ref, bpw_ref,
                 o_ref, ubuf_ref, cbuf_ref, *, rc, lc):
    ts, c = uval_ref.shape

    @pl.when(pl.program_id(1) == 0)
    def _():
        ubuf_ref[0:HALO, :] = jnp.zeros((HALO, c), F32)

    @pl.when(pl.program_id(1) != 0)
    def _():
        ubuf_ref[0:HALO, :] = ubuf_ref[ts:ts + HALO, :]

    a = uval_ref[...].astype(F32) + glub_ref[:, 0:c]
    g = ugate_ref[...].astype(F32) + glub_ref[:, c:2 * c]
    ubuf_ref[HALO:HALO + ts, :] = a * _sigmoid(g)

    first = HALO - (CONV_K - 1)
    for r0 in range(0, ts, rc):
        for c0 in range(0, c, lc):
            acc = jnp.zeros((rc, lc), F32)
            for j in range(CONV_K):
                acc = acc + dww_ref[j:j + 1, c0:c0 + lc] * ubuf_ref[r0 + first + j:r0 + first + j + rc, c0:c0 + lc]
            cbuf_ref[r0:r0 + rc, c0:c0 + lc] = acc + dwb_ref[:, c0:c0 + lc]

    y = cbuf_ref[...]
    mu = jnp.mean(y, axis=-1, keepdims=True)
    yc = y - mu
    var = jnp.mean(yc * yc, axis=-1, keepdims=True)
    yn = yc * lax.rsqrt(var + EPS) * lng_ref[...] + lnb_ref[...]
    act = _silu(yn).astype(BF16)
    pw = jnp.dot(act, wpw_ref[...], preferred_element_type=F32) + bpw_ref[...]
    o_ref[...] = (pw * _silu(cgate_ref[...].astype(F32))).astype(BF16)


def _conv_module(z, glub, dww, dwb, lng, lnb, wpw, bpw, batch, seq, ts):
    c = wpw.shape[0]
    tiles_per_seq = seq // ts

    def zcol(offset):
        return pl.BlockSpec((ts, c), lambda b, i: (b * tiles_per_seq + i, offset // c))

    def full(arr):
        return pl.BlockSpec(arr.shape, lambda b, i: (0,) * arr.ndim)

    return pl.pallas_call(
        functools.partial(_conv_kernel, rc=32, lc=512),
        grid=(batch, tiles_per_seq),
        in_specs=[zcol(Z_CONV_IN), zcol(Z_CONV_IN + c), zcol(Z_CONV_GATE),
                  full(glub), full(dww), full(dwb), full(lng), full(lnb), full(wpw), full(bpw)],
        out_specs=pl.BlockSpec((ts, c), lambda b, i: (b * tiles_per_seq + i, 0)),
        out_shape=jax.ShapeDtypeStruct((batch * seq, c), BF16),
        scratch_shapes=[pltpu.VMEM((HALO + ts, c), F32), pltpu.VMEM((ts, c), F32)],
        compiler_params=_params(("parallel", "arbitrary")),
        name="conv_module",
    )(z, z, z, glub, dww, dwb, lng, lnb, wpw, bpw)


def _outproj_kernel(mla_ref, conv_ref, x_ref, gate_ref, wa_ref, wb_ref, o_ref):
    y = jnp.dot(mla_ref[...], wa_ref[...], preferred_element_type=F32)
    y = y + jnp.dot(conv_ref[...], wb_ref[...], preferred_element_type=F32)
    o_ref[...] = x_ref[...] + gate_ref[0] * y


def _outproj(mla, conv, x2, mod3, w_a, w_b, seq, tm):
    t, d = x2.shape
    half = mla.shape[1]
    tiles_per_seq = seq // tm
    return pl.pallas_call(
        _outproj_kernel,
        grid=(t // tm,),
        in_specs=[
            pl.BlockSpec((tm, half), lambda i: (i, 0)),
            pl.BlockSpec((tm, half), lambda i: (i, 0)),
            pl.BlockSpec((tm, d), lambda i: (i, 0)),
            pl.BlockSpec((1, 1, d), lambda i: (i // tiles_per_seq, 0, 2)),
            pl.BlockSpec((half, d), lambda i: (0, 0)),
            pl.BlockSpec((half, d), lambda i: (0, 0)),
        ],
        out_specs=pl.BlockSpec((tm, d), lambda i: (i, 0)),
        out_shape=jax.ShapeDtypeStruct((t, d), F32),
        compiler_params=_params(("parallel",)),
        name="outproj",
    )(mla, conv, x2, mod3, w_a, w_b)


def _tile(n, want):
    t = min(n, want)
    assert n % t == 0, (n, t)
    return t


def kernel(x, c, positions, ada_w, ada_b, norm_g, w_in, q_lat_g, w_q_up, kv_lat_g, w_kv_up, q_norm_g, k_norm_g,
           glu_b, dw_w, dw_b, conv_ln_g, conv_ln_b, w_pw, b_pw, w_out):
    batch, seq, d = x.shape
    depth = ada_w.shape[0]
    t = batch * seq
    d_mla = N_HEADS * V_DIM
    d_conv = w_pw.shape[1]
    assert w_in.shape[2] == Q_LORA + KV_LORA + ROPE_DIM + d_mla + 3 * d_conv
    assert d_mla == 1024 and d_conv == 1024 and d == 2048

    inv_freq = 1.0 / (ROPE_THETA ** (jnp.arange(0, ROPE_DIM, 2, dtype=F32) / ROPE_DIM))
    inv128 = jnp.tile(inv_freq, LANES // (ROPE_DIM // 2))[None, :]
    sign64 = jnp.concatenate([-jnp.ones((ROPE_DIM // 2,), F32), jnp.ones((ROPE_DIM // 2,), F32)])
    sign128 = jnp.tile(sign64, LANES // ROPE_DIM)[None, :]
    pos_b = jnp.broadcast_to(positions.reshape(t, 1), (t, LANES))
    cos_t, sin_t = _rope_tables(pos_b, inv128, sign128, _tile(t, 2048))

    c_rows = 8
    c_pad = jnp.zeros((c_rows, d), F32).at[:batch].set(c)
    mod = _adaln(c_pad, ada_w, ada_b.reshape(depth, 1, 3 * d), _tile(3 * d, 1024))

    x2 = x.reshape(t, d)
    s0 = Q_LORA
    s1 = s0 + KV_LORA
    s2 = s1 + ROPE_DIM
    s3 = s2 + d_mla
    s4 = s3 + 2 * d_conv
    for l in range(depth):
        mod3 = mod[l].reshape(c_rows, 1, 3 * d)
        wl = w_in[l]
        w_z = jnp.concatenate(
            [wl[:, s3:s4], wl[:, s2:s3], wl[:, s4:], wl[:, :s0], wl[:, s0:s1], wl[:, s1:s2],
             jnp.zeros((d, Z_COLS - wl.shape[1]), F32)], axis=1).astype(BF16)
        wq = w_q_up[l].reshape(Q_LORA, N_HEADS, QK_DIM)
        wq = jnp.concatenate([wq[:, :, :NOPE_DIM].reshape(Q_LORA, N_HEADS * NOPE_DIM),
                              wq[:, :, NOPE_DIM:].reshape(Q_LORA, N_HEADS * ROPE_DIM)], axis=1).astype(BF16)
        wkv = w_kv_up[l].astype(BF16)
        gq, gk = q_norm_g[l], k_norm_g[l]
        gqn, gqr = gq[None, :NOPE_DIM], jnp.tile(gq[NOPE_DIM:], 2)[None, :]
        gkn = gk[None, :NOPE_DIM]
        gkr = jnp.concatenate([gk[NOPE_DIM:], jnp.zeros((LANES - ROPE_DIM,), F32)])[None, :]

        z = _inproj(x2, mod3, norm_g[l][None, :], w_z, seq, _tile(seq, 1024), 1280)
        q, k, v = _qkv_up(z, cos_t, sin_t, q_lat_g[l][None, :], kv_lat_g[l][None, :], wq, wkv,
                          gqn, gqr, gkn, gkr, batch, seq, _tile(seq, 512))
        mla = _attention(q, k, v, z, _tile(seq, 512), _tile(seq, 512))
        dww = jnp.concatenate([dw_w[l], jnp.zeros((HALO - CONV_K, d_conv), F32)], axis=0)
        conv = _conv_module(z, glu_b[l][None, :], dww, dw_b[l][None, :], conv_ln_g[l][None, :],
                            conv_ln_b[l][None, :], w_pw[l].astype(BF16), b_pw[l][None, :], batch, seq,
                            _tile(seq, 512))
        wo = w_out[l].astype(BF16)
        x2 = _outproj(mla, conv, x2, mod3, wo[:d_mla], wo[d_mla:], seq, _tile(seq, 512))
    return x2.reshape(batch, seq, d)
```

```python
import functools
import math

import jax
import jax.numpy as jnp
from jax import lax
from jax.experimental import pallas as pl
from jax.experimental.pallas import tpu as pltpu

F32 = jnp.float32
BF16 = jnp.bfloat16

N_HEADS = 8
NOPE_DIM = 128
ROPE_DIM = 64
V_DIM = 128
QK_DIM = NOPE_DIM + ROPE_DIM
Q_LORA = 512
KV_LORA = 256
ROPE_THETA = 10000.0
CONV_K = 31
EPS = 1e-6
LANES = 128
BF16_SUBLANES = 16
VT_ROWS = V_DIM + BF16_SUBLANES
_NT = (((1,), (1,)), ((), ()))

Z_CONV_IN = 0
Z_MLA_GATE = 2048
Z_CONV_GATE = 3072
Z_Q_LAT = 4096
Z_KV_LAT = 4608
Z_K_ROPE = 4864
Z_COLS = 5120

VMEM_LIMIT = 56 * 1024 * 1024


def _sigmoid(x):
    return 1.0 / (1.0 + jnp.exp(-x))


def _silu(x):
    return x * _sigmoid(x)


def _params(sem, vmem=VMEM_LIMIT):
    return pltpu.CompilerParams(dimension_semantics=sem, vmem_limit_bytes=vmem)


def _rope_table_kernel(pos_ref, inv_ref, sign_ref, cos_ref, sin_ref):
    ang = pos_ref[...].astype(F32) * inv_ref[...]
    cos_ref[...] = jnp.cos(ang)
    sin_ref[...] = jnp.sin(ang) * sign_ref[...]


def _rope_tables(pos_b, inv128, sign128, tr):
    t = pos_b.shape[0]
    row = pl.BlockSpec((tr, LANES), lambda i: (i, 0))
    vec = pl.BlockSpec((1, LANES), lambda i: (0, 0))
    return pl.pallas_call(
        _rope_table_kernel,
        grid=(t // tr,),
        in_specs=[row, vec, vec],
        out_specs=[row, row],
        out_shape=[jax.ShapeDtypeStruct((t, LANES), F32)] * 2,
        compiler_params=_params(("parallel",)),
        name="rope_tables",
    )(pos_b, inv128, sign128)


def _adaln_kernel(c_ref, w_ref, b_ref, o_ref):
    c_act = _silu(c_ref[...]).astype(BF16)
    acc = jnp.dot(c_act, w_ref[0].astype(BF16), preferred_element_type=F32)
    o_ref[0] = acc + b_ref[0]


def _adaln(c_pad, ada_w, ada_b3, tn):
    depth, d, n = ada_w.shape
    rows = c_pad.shape[0]
    return pl.pallas_call(
        _adaln_kernel,
        grid=(depth, n // tn),
        in_specs=[
            pl.BlockSpec((rows, d), lambda l, j: (0, 0)),
            pl.BlockSpec((1, d, tn), lambda l, j: (l, 0, j)),
            pl.BlockSpec((1, 1, tn), lambda l, j: (l, 0, j)),
        ],
        out_specs=pl.BlockSpec((1, rows, tn), lambda l, j: (l, 0, j)),
        out_shape=jax.ShapeDtypeStruct((depth, rows, n), F32),
        compiler_params=_params(("parallel", "parallel")),
        name="adaln_mod",
    )(c_pad, ada_w, ada_b3)


def _inproj_kernel(x_ref, shift_ref, scale_ref, g_ref, w_ref, z_ref, h_ref, *, rc):
    tm, d = x_ref.shape

    @pl.when(pl.program_id(1) == 0)
    def _():
        mult = g_ref[...] * (1.0 + scale_ref[0])
        shift = shift_ref[0]

        def body(r, carry):
            rows = pl.ds(pl.multiple_of(r * rc, rc), rc)
            x = x_ref[rows, :]
            inv = lax.rsqrt(jnp.sum(x * x, axis=-1, keepdims=True) * (1.0 / d) + EPS)
            h_ref[rows, :] = (x * inv * mult + shift).astype(BF16)
            return carry

        lax.fori_loop(0, tm // rc, body, 0)

    z_ref[...] = jnp.dot(h_ref[...], w_ref[...], preferred_element_type=F32).astype(BF16)


def _inproj(x2, mod3, g, w, seq, tm, tn):
    t, d = x2.shape
    ncol = w.shape[1]
    tiles_per_seq = seq // tm

    def mod_spec(chunk):
        return pl.BlockSpec((1, 1, d), lambda i, j: (i // tiles_per_seq, 0, chunk))

    return pl.pallas_call(
        functools.partial(_inproj_kernel, rc=64),
        grid=(t // tm, ncol // tn),
        in_specs=[
            pl.BlockSpec((tm, d), lambda i, j: (i, 0)),
            mod_spec(0),
            mod_spec(1),
            pl.BlockSpec((1, d), lambda i, j: (0, 0)),
            pl.BlockSpec((d, tn), lambda i, j: (0, j)),
        ],
        out_specs=pl.BlockSpec((tm, tn), lambda i, j: (i, j)),
        out_shape=jax.ShapeDtypeStruct((t, ncol), BF16),
        scratch_shapes=[pltpu.VMEM((tm, d), BF16)],
        compiler_params=_params(("parallel", "arbitrary")),
        name="inproj",
    )(x2, mod3, mod3, g, w)


def _rms_rows(x, g):
    inv = lax.rsqrt(jnp.mean(x * x, axis=-1, keepdims=True) + EPS)
    return x * inv * g


def _rotate_half_pairs(x):
    lane = lax.broadcasted_iota(jnp.int32, x.shape, 1)
    first = (lane % ROPE_DIM) < (ROPE_DIM // 2)
    return jnp.where(first, pltpu.roll(x, LANES - ROPE_DIM // 2, 1), pltpu.roll(x, ROPE_DIM // 2, 1))


def _qkv_kernel(qlat_ref, kvlat_ref, krope_ref, cos_ref, sin_ref, gql_ref, gkvl_ref, wq_ref, wk_ref, wvt_ref,
                gqn_ref, gqr_ref, gkn_ref, gkr_ref, q_ref, k_ref, vt_ref):
    cos = cos_ref[...]
    sin = sin_ref[...]
    lane = lax.broadcasted_iota(jnp.int32, cos.shape, 1)
    low_half = lane < ROPE_DIM

    qn = _rms_rows(qlat_ref[...].astype(F32), gql_ref[...]).astype(BF16)
    q = jnp.dot(qn, wq_ref[...], preferred_element_type=F32)
    kvn = _rms_rows(kvlat_ref[...].astype(F32), gkvl_ref[...]).astype(BF16)
    kn = jnp.dot(kvn, wk_ref[...], preferred_element_type=F32)
    vt = lax.dot_general(wvt_ref[...], kvn, _NT, preferred_element_type=F32)

    kr = krope_ref[...].astype(F32)
    kr_ss = jnp.sum(kr * kr, axis=-1, keepdims=True)
    krg = kr * gkr_ref[...]
    kr_rot = krg * cos + _rotate_half_pairs(krg) * sin

    q_scale = math.log2(math.e) / math.sqrt(QK_DIM)
    rope_base = N_HEADS * NOPE_DIM
    for pair in range(N_HEADS // 2):
        qr = q[:, rope_base + pair * LANES: rope_base + (pair + 1) * LANES]
        qr_sq = qr * qr
        qrg = qr * gqr_ref[...]
        qr_rot = qrg * cos + _rotate_half_pairs(qrg) * sin
        for sub in range(2):
            h = 2 * pair + sub
            q_nope = q[:, h * NOPE_DIM:(h + 1) * NOPE_DIM]
            mask = low_half if sub == 0 else jnp.logical_not(low_half)
            ss = (jnp.sum(q_nope * q_nope, axis=-1, keepdims=True)
                  + jnp.sum(jnp.where(mask, qr_sq, 0.0), axis=-1, keepdims=True))
            inv = lax.rsqrt(ss * (1.0 / QK_DIM) + EPS) * q_scale
            q_ref[0, h, :, 0:NOPE_DIM] = (q_nope * inv * gqn_ref[...]).astype(BF16)
            q_ref[0, h, :, NOPE_DIM:QK_DIM] = (
                qr_rot[:, sub * ROPE_DIM:(sub + 1) * ROPE_DIM] * inv).astype(BF16)

    ones_rows = jnp.ones((VT_ROWS - V_DIM, vt.shape[1]), BF16)
    for h in range(N_HEADS):
        k_nope = kn[:, h * NOPE_DIM:(h + 1) * NOPE_DIM]
        ss = jnp.sum(k_nope * k_nope, axis=-1, keepdims=True) + kr_ss
        inv = lax.rsqrt(ss * (1.0 / QK_DIM) + EPS)
        k_ref[0, h, :, 0:NOPE_DIM] = (k_nope * inv * gkn_ref[...]).astype(BF16)
        k_ref[0, h, :, NOPE_DIM:QK_DIM] = (kr_rot[:, 0:ROPE_DIM] * inv).astype(BF16)
        vt_ref[0, h, 0:V_DIM, :] = vt[h * V_DIM:(h + 1) * V_DIM, :].astype(BF16)
        vt_ref[0, h, V_DIM:VT_ROWS, :] = ones_rows


def _qkv_up(z, cos_t, sin_t, gql, gkvl, wq, wk, wvt, gqn, gqr, gkn, gkr, batch, seq, tm):
    tiles_per_seq = seq // tm

    def zcol(width, offset):
        return pl.BlockSpec((tm, width), lambda i: (i, offset // width))

    def full(arr):
        return pl.BlockSpec(arr.shape, lambda i: (0,) * arr.ndim)

    head_out = pl.BlockSpec((1, N_HEADS, tm, QK_DIM), lambda i: (i // tiles_per_seq, 0, i % tiles_per_seq, 0))
    vt_out = pl.BlockSpec((1, N_HEADS, VT_ROWS, tm), lambda i: (i // tiles_per_seq, 0, 0, i % tiles_per_seq))
    row = pl.BlockSpec((tm, LANES), lambda i: (i, 0))
    return pl.pallas_call(
        _qkv_kernel,
        grid=(batch * tiles_per_seq,),
        in_specs=[zcol(Q_LORA, Z_Q_LAT), zcol(KV_LORA, Z_KV_LAT), zcol(LANES, Z_K_ROPE), row, row,
                  full(gql), full(gkvl), full(wq), full(wk), full(wvt), full(gqn), full(gqr), full(gkn), full(gkr)],
        out_specs=[head_out, head_out, vt_out],
        out_shape=[jax.ShapeDtypeStruct((batch, N_HEADS, seq, QK_DIM), BF16),
                   jax.ShapeDtypeStruct((batch, N_HEADS, seq, QK_DIM), BF16),
                   jax.ShapeDtypeStruct((batch, N_HEADS, VT_ROWS, seq), BF16)],
        compiler_params=_params(("parallel",)),
        name="qkv_up",
    )(z, z, z, cos_t, sin_t, gql, gkvl, wq, wk, wvt, gqn, gqr, gkn, gkr)


def _attn_kernel(q_ref, k_ref, vt_ref, gate_ref, o_ref, m_ref, alpha_ref, acc_ref, p_ref, *, tq, tk, qc):
    qi = pl.program_id(2)
    m_ref[...] = jnp.full(m_ref.shape, -jnp.inf, F32)
    acc_ref[...] = jnp.zeros(acc_ref.shape, F32)
    alpha_ref[...] = jnp.ones(alpha_ref.shape, F32)
    p_ref[...] = jnp.zeros(p_ref.shape, BF16)
    chains = tuple(range(tq // qc))

    def qcols(c):
        return slice(c * qc, (c + 1) * qc)

    def keys_of(kb):
        return pl.ds(pl.multiple_of(kb * tk, tk), tk)

    def qk(k, c):
        return lax.dot_general(k, q_ref[0, 0, qcols(c), :], _NT, preferred_element_type=F32)

    def softmax(c, parts):
        m_prev = m_ref[:, qcols(c)]
        m_new = m_prev
        for _, st in parts:
            m_new = jnp.maximum(m_new, jnp.max(st, axis=0, keepdims=True))
        alpha_ref[:, qcols(c)] = jnp.exp2(m_prev - m_new)
        for r0, st in parts:
            p_ref[r0:r0 + st.shape[0], qcols(c)] = jnp.exp2(st - m_new).astype(BF16)
        m_ref[:, qcols(c)] = m_new

    def pv(vt, c, rows):
        acc_ref[:, qcols(c)] = (alpha_ref[:, qcols(c)] * acc_ref[:, qcols(c)]
                                + jnp.dot(vt, p_ref[0:rows, qcols(c)], preferred_element_type=F32))

    def full_block(kb, prev_kb):
        k = k_ref[0, 0, keys_of(kb), :]
        sts = [qk(k, c) for c in chains]
        if prev_kb is not None:
            vt = vt_ref[0, 0, :, keys_of(prev_kb)]
            for c in chains:
                pv(vt, c, tk)
        for c, st in zip(chains, sts):
            softmax(c, [(0, st)])

    n_full = (qi * tq) // tk

    @pl.when(n_full > 0)
    def _():
        full_block(0, None)

    lax.fori_loop(1, n_full, lambda kb, carry: (full_block(kb, kb - 1), carry)[1], 0)

    d0 = pl.multiple_of(qi * tq, tq)
    sts = [qk(k_ref[0, 0, pl.ds(d0, (c + 1) * qc), :], c) for c in chains]
    vt = vt_ref[0, 0, :, keys_of(jnp.maximum(n_full - 1, 0))]
    for c in chains:
        pv(vt, c, tk)
    tri = (lax.broadcasted_iota(jnp.int32, (qc, qc), 0) <= lax.broadcasted_iota(jnp.int32, (qc, qc), 1))
    for c, st in zip(chains, sts):
        parts = [(c * qc, jnp.where(tri, st[c * qc:, :], -jnp.inf))]
        if c:
            parts.insert(0, (0, st[:c * qc, :]))
        softmax(c, parts)
    for c in chains:
        pv(vt_ref[0, 0, :, pl.ds(d0, (c + 1) * qc)], c, (c + 1) * qc)

    acc = acc_ref[...]
    out_t = acc[0:V_DIM, :] / acc[V_DIM:V_DIM + 1, :]
    o_ref[...] = (out_t.T * _silu(gate_ref[...].astype(F32))).astype(BF16)


def _attention(q, k, vt, z, tq, tk):
    batch, heads, seq, _ = q.shape
    nq = seq // tq
    gate_col0 = Z_MLA_GATE // V_DIM
    return pl.pallas_call(
        functools.partial(_attn_kernel, tq=tq, tk=tk, qc=min(tq, 256)),
        grid=(batch, heads, nq),
        in_specs=[
            pl.BlockSpec((1, 1, tq, QK_DIM), lambda b, h, i: (b, h, i, 0)),
            pl.BlockSpec((1, 1, seq, QK_DIM), lambda b, h, i: (b, h, 0, 0)),
            pl.BlockSpec((1, 1, VT_ROWS, seq), lambda b, h, i: (b, h, 0, 0)),
            pl.BlockSpec((tq, V_DIM), lambda b, h, i: (b * nq + i, gate_col0 + h)),
        ],
        out_specs=pl.BlockSpec((tq, V_DIM), lambda b, h, i: (b * nq + i, h)),
        out_shape=jax.ShapeDtypeStruct((batch * seq, heads * V_DIM), BF16),
        scratch_shapes=[pltpu.VMEM((1, tq), F32), pltpu.VMEM((1, tq), F32), pltpu.VMEM((VT_ROWS, tq), F32),
                        pltpu.VMEM((max(tk, tq), tq), BF16)],
        compiler_params=_params(("parallel", "parallel", "arbitrary")),
        name="mla_attention",
    )(q, k, vt, z)


HALO = 32
SUBLANES = 8
PITCH = 100
PHASES = 16


def _conv_kernel(uval_ref, ugate_ref, cgate_ref, glub_ref, dww_ref, dwb_ref, lng_ref, lnb_ref, wpw_ref, bpw_ref,
                 o_ref, ubuf_ref, cbuf_ref, carry_ref):
    ts, c = uval_ref.shape
    seg = ts // SUBLANES
    n_slabs = c // LANES
    first = HALO - (CONV_K - 1)

    @pl.when(pl.program_id(1) == 0)
    def _():
        carry_ref[...] = jnp.zeros(carry_ref.shape, F32)

    a = uval_ref[...].astype(F32) + glub_ref[:, 0:c]
    g = ugate_ref[...].astype(F32) + glub_ref[:, c:2 * c]
    u = a * _sigmoid(g)
    for s in range(n_slabs):
        us = u[:, s * LANES:(s + 1) * LANES]
        ubuf_ref[s, first:HALO, :] = carry_ref[s, first:HALO, :]
        for k in range(SUBLANES):
            ubuf_ref[s, k * PITCH + HALO:k * PITCH + HALO + seg, :] = us[k * seg:(k + 1) * seg]
            if k + 1 < SUBLANES:
                ubuf_ref[s, (k + 1) * PITCH + first:(k + 1) * PITCH + HALO, :] = (
                    us[(k + 1) * seg - (HALO - first):(k + 1) * seg])
        carry_ref[s, first:HALO, :] = us[ts - (HALO - first):ts]

    def slab_body(s, carry):
        bias = dwb_ref[s]

        def group_body(gi, carry2):
            b0 = gi * PHASES
            accs = [None] * PHASES
            for r in range(CONV_K - 1 + PHASES):
                rows = ubuf_ref[s, pl.ds(b0 + first + r, SUBLANES, stride=PITCH), :]
                for ph in range(PHASES):
                    j = r - ph
                    if 0 <= j < CONV_K:
                        term = dww_ref[s, j:j + 1, :] * rows
                        accs[ph] = term if accs[ph] is None else accs[ph] + term
            for ph in range(PHASES):
                cbuf_ref[s, pl.ds(b0 + ph, SUBLANES, stride=seg), :] = accs[ph] + bias
            return carry2

        return lax.fori_loop(0, seg // PHASES, group_body, carry)

    lax.fori_loop(0, n_slabs, slab_body, 0)

    y = jnp.concatenate([cbuf_ref[s] for s in range(n_slabs)], axis=-1)
    mu = jnp.mean(y, axis=-1, keepdims=True)
    yc = y - mu
    var = jnp.mean(yc * yc, axis=-1, keepdims=True)
    yn = yc * lax.rsqrt(var + EPS) * lng_ref[...] + lnb_ref[...]
    act = _silu(yn).astype(BF16)
    pw = jnp.dot(act, wpw_ref[...], preferred_element_type=F32) + bpw_ref[...]
    o_ref[...] = (pw * _silu(cgate_ref[...].astype(F32))).astype(BF16)


def _conv_module(z, glub, dww, dwb, lng, lnb, wpw, bpw, batch, seq, ts):
    c = wpw.shape[0]
    n_slabs = c // LANES
    tiles_per_seq = seq // ts
    assert ts % (SUBLANES * PHASES) == 0 and HALO + ts // SUBLANES <= PITCH

    def zcol(offset):
        return pl.BlockSpec((ts, c), lambda b, i: (b * tiles_per_seq + i, offset // c))

    def full(arr):
        return pl.BlockSpec(arr.shape, lambda b, i: (0,) * arr.ndim)

    return pl.pallas_call(
        _conv_kernel,
        grid=(batch, tiles_per_seq),
        in_specs=[zcol(Z_CONV_IN), zcol(Z_CONV_IN + c), zcol(Z_CONV_GATE),
                  full(glub), full(dww), full(dwb), full(lng), full(lnb), full(wpw), full(bpw)],
        out_specs=pl.BlockSpec((ts, c), lambda b, i: (b * tiles_per_seq + i, 0)),
        out_shape=jax.ShapeDtypeStruct((batch * seq, c), BF16),
        scratch_shapes=[pltpu.VMEM((n_slabs, SUBLANES * PITCH, LANES), F32), pltpu.VMEM((n_slabs, ts, LANES), F32),
                        pltpu.VMEM((n_slabs, HALO, LANES), F32)],
        compiler_params=_params(("parallel", "arbitrary")),
        name="conv_module",
    )(z, z, z, glub, dww, dwb, lng, lnb, wpw, bpw)


def _slabs(w):
    rows, c = w.shape
    return w.reshape(rows, c // LANES, LANES).transpose(1, 0, 2)


def _outproj_kernel(mla_ref, conv_ref, x_ref, gate_ref, wa_ref, wb_ref, o_ref):
    y = jnp.dot(mla_ref[...], wa_ref[...], preferred_element_type=F32)
    y = y + jnp.dot(conv_ref[...], wb_ref[...], preferred_element_type=F32)
    o_ref[...] = x_ref[...] + gate_ref[0] * y


def _outproj(mla, conv, x2, mod3, w_a, w_b, seq, tm):
    t, d = x2.shape
    half = mla.shape[1]
    tiles_per_seq = seq // tm
    return pl.pallas_call(
        _outproj_kernel,
        grid=(t // tm,),
        in_specs=[
            pl.BlockSpec((tm, half), lambda i: (i, 0)),
            pl.BlockSpec((tm, half), lambda i: (i, 0)),
            pl.BlockSpec((tm, d), lambda i: (i, 0)),
            pl.BlockSpec((1, 1, d), lambda i: (i // tiles_per_seq, 0, 2)),
            pl.BlockSpec((half, d), lambda i: (0, 0)),
            pl.BlockSpec((half, d), lambda i: (0, 0)),
        ],
        out_specs=pl.BlockSpec((tm, d), lambda i: (i, 0)),
        out_shape=jax.ShapeDtypeStruct((t, d), F32),
        compiler_params=_params(("parallel",)),
        name="outproj",
    )(mla, conv, x2, mod3, w_a, w_b)


def _tile(n, want):
    t = min(n, want)
    assert n % t == 0, (n, t)
    return t


def kernel(x, c, positions, ada_w, ada_b, norm_g, w_in, q_lat_g, w_q_up, kv_lat_g, w_kv_up, q_norm_g, k_norm_g,
           glu_b, dw_w, dw_b, conv_ln_g, conv_ln_b, w_pw, b_pw, w_out):
    batch, seq, d = x.shape
    depth = ada_w.shape[0]
    t = batch * seq
    d_mla = N_HEADS * V_DIM
    d_conv = w_pw.shape[1]
    assert w_in.shape[2] == Q_LORA + KV_LORA + ROPE_DIM + d_mla + 3 * d_conv
    assert d_mla == 1024 and d_conv == 1024 and d == 2048

    inv_freq = 1.0 / (ROPE_THETA ** (jnp.arange(0, ROPE_DIM, 2, dtype=F32) / ROPE_DIM))
    inv128 = jnp.tile(inv_freq, LANES // (ROPE_DIM // 2))[None, :]
    sign64 = jnp.concatenate([-jnp.ones((ROPE_DIM // 2,), F32), jnp.ones((ROPE_DIM // 2,), F32)])
    sign128 = jnp.tile(sign64, LANES // ROPE_DIM)[None, :]
    pos_b = jnp.broadcast_to(positions.reshape(t, 1), (t, LANES))
    cos_t, sin_t = _rope_tables(pos_b, inv128, sign128, _tile(t, 2048))

    c_rows = 8
    c_pad = jnp.zeros((c_rows, d), F32).at[:batch].set(c)
    mod = _adaln(c_pad, ada_w, ada_b.reshape(depth, 1, 3 * d), _tile(3 * d, 1024))

    x2 = x.reshape(t, d)
    s0 = Q_LORA
    s1 = s0 + KV_LORA
    s2 = s1 + ROPE_DIM
    s3 = s2 + d_mla
    s4 = s3 + 2 * d_conv
    for l in range(depth):
        mod3 = mod[l].reshape(c_rows, 1, 3 * d)
        wl = w_in[l]
        w_z = jnp.concatenate(
            [wl[:, s3:s4], wl[:, s2:s3], wl[:, s4:], wl[:, :s0], wl[:, s0:s1], wl[:, s1:s2],
             jnp.zeros((d, Z_COLS - wl.shape[1]), F32)], axis=1).astype(BF16)
        wq = w_q_up[l].reshape(Q_LORA, N_HEADS, QK_DIM)
        wq = jnp.concatenate([wq[:, :, :NOPE_DIM].reshape(Q_LORA, N_HEADS * NOPE_DIM),
                              wq[:, :, NOPE_DIM:].reshape(Q_LORA, N_HEADS * ROPE_DIM)], axis=1).astype(BF16)
        wkv = w_kv_up[l].reshape(KV_LORA, N_HEADS, NOPE_DIM + V_DIM)
        wk = wkv[:, :, :NOPE_DIM].reshape(KV_LORA, N_HEADS * NOPE_DIM).astype(BF16)
        wvt = wkv[:, :, NOPE_DIM:].reshape(KV_LORA, N_HEADS * V_DIM).T.astype(BF16)
        gq, gk = q_norm_g[l], k_norm_g[l]
        gqn, gqr = gq[None, :NOPE_DIM], jnp.tile(gq[NOPE_DIM:], 2)[None, :]
        gkn = gk[None, :NOPE_DIM]
        gkr = jnp.concatenate([gk[NOPE_DIM:], jnp.zeros((LANES - ROPE_DIM,), F32)])[None, :]

        z = _inproj(x2, mod3, norm_g[l][None, :], w_z, seq, _tile(seq, 1024), 1280)
        q, k, vt = _qkv_up(z, cos_t, sin_t, q_lat_g[l][None, :], kv_lat_g[l][None, :], wq, wk, wvt,
                           gqn, gqr, gkn, gkr, batch, seq, _tile(seq, 512))
        mla = _attention(q, k, vt, z, _tile(seq, 1024), _tile(seq, 1024))
        dww = jnp.concatenate([dw_w[l], jnp.zeros((HALO - CONV_K, d_conv), F32)], axis=0)
        conv = _conv_module(z, glu_b[l][None, :], _slabs(dww), _slabs(dw_b[l][None, :]), conv_ln_g[l][None, :],
                            conv_ln_b[l][None, :], w_pw[l].astype(BF16), b_pw[l][None, :], batch, seq,
                            _tile(seq, 512))
        wo = w_out[l].astype(BF16)
        x2 = _outproj(mla, conv, x2, mod3, wo[:d_mla], wo[d_mla:], seq, _tile(seq, 512))
    return x2.reshape(batch, seq, d)
```

```python
import functools
import math

import jax
import jax.numpy as jnp
from jax import lax
from jax.experimental import pallas as pl
from jax.experimental.pallas import tpu as pltpu

F32 = jnp.float32
BF16 = jnp.bfloat16

N_HEADS = 8
NOPE_DIM = 128
ROPE_DIM = 64
V_DIM = 128
QK_DIM = NOPE_DIM + ROPE_DIM
Q_LORA = 512
KV_LORA = 256
ROPE_THETA = 10000.0
CONV_K = 31
EPS = 1e-6
LANES = 128
BF16_SUBLANES = 16
VT_ROWS = V_DIM + BF16_SUBLANES
_NT = (((1,), (1,)), ((), ()))

Z_CONV_IN = 0
Z_MLA_GATE = 2048
Z_CONV_GATE = 3072
Z_Q_LAT = 4096
Z_KV_LAT = 4608
Z_K_ROPE = 4864
Z_COLS = 5120

VMEM_LIMIT = 56 * 1024 * 1024


def _sigmoid(x):
    return 1.0 / (1.0 + jnp.exp(-x))


def _silu(x):
    return x * _sigmoid(x)


def _params(sem, vmem=VMEM_LIMIT):
    return pltpu.CompilerParams(dimension_semantics=sem, vmem_limit_bytes=vmem)


def _rope_table_kernel(pos_ref, inv_ref, sign_ref, post_ref, invt_ref, cos_ref, sin_ref, cost_ref, sint_ref):
    ang = pos_ref[...].astype(F32) * inv_ref[...]
    cos_ref[...] = jnp.cos(ang)
    sin_ref[...] = jnp.sin(ang) * sign_ref[...]
    angt = post_ref[...].astype(F32) * invt_ref[...]
    cost_ref[...] = jnp.cos(angt)
    sint_ref[...] = jnp.sin(angt)


def _rope_tables(pos_b, inv128, sign128, pos_t, inv_t):
    t = pos_b.shape[0]
    half, tr = inv_t.shape
    row = pl.BlockSpec((tr, LANES), lambda i: (i, 0))
    vec = pl.BlockSpec((1, LANES), lambda i: (0, 0))
    col = pl.BlockSpec((half, tr), lambda i: (0, i))
    return pl.pallas_call(
        _rope_table_kernel,
        grid=(t // tr,),
        in_specs=[row, vec, vec, col, pl.BlockSpec((half, tr), lambda i: (0, 0))],
        out_specs=[row, row, col, col],
        out_shape=[jax.ShapeDtypeStruct((t, LANES), F32)] * 2 + [jax.ShapeDtypeStruct((half, t), F32)] * 2,
        compiler_params=_params(("parallel",)),
        name="rope_tables",
    )(pos_b, inv128, sign128, pos_t, inv_t)


def _adaln_kernel(c_ref, w_ref, b_ref, o_ref):
    c_act = _silu(c_ref[...]).astype(BF16)
    acc = jnp.dot(c_act, w_ref[0].astype(BF16), preferred_element_type=F32)
    o_ref[0] = acc + b_ref[0]


def _adaln(c_pad, ada_w, ada_b3, tn):
    depth, d, n = ada_w.shape
    rows = c_pad.shape[0]
    return pl.pallas_call(
        _adaln_kernel,
        grid=(depth, n // tn),
        in_specs=[
            pl.BlockSpec((rows, d), lambda l, j: (0, 0)),
            pl.BlockSpec((1, d, tn), lambda l, j: (l, 0, j)),
            pl.BlockSpec((1, 1, tn), lambda l, j: (l, 0, j)),
        ],
        out_specs=pl.BlockSpec((1, rows, tn), lambda l, j: (l, 0, j)),
        out_shape=jax.ShapeDtypeStruct((depth, rows, n), F32),
        compiler_params=_params(("parallel", "parallel")),
        name="adaln_mod",
    )(c_pad, ada_w, ada_b3)


def _inproj_kernel(x_ref, shift_ref, scale_ref, g_ref, w_ref, z_ref, h0_ref, h1_ref, *, nm, nn):
    i, j = pl.program_id(0), pl.program_id(1)
    tm, d = x_ref.shape
    rows_per_step = tm // nn
    h_refs = (h0_ref, h1_ref)

    def normalise(h_ref):
        rows = pl.ds(pl.multiple_of(j * rows_per_step, rows_per_step), rows_per_step)
        x = x_ref[rows, :]
        inv = lax.rsqrt(jnp.sum(x * x, axis=-1, keepdims=True) * (1.0 / d) + EPS)
        h_ref[rows, :] = (x * inv * (g_ref[...] * (1.0 + scale_ref[0])) + shift_ref[0]).astype(BF16)

    def project(h_ref):
        z_ref[...] = jnp.dot(h_ref[...], w_ref[...], preferred_element_type=F32).astype(BF16)

    @pl.when(i == 0)
    def _():
        normalise(h_refs[0])

    for parity in range(2):
        @pl.when(jnp.logical_and(jnp.logical_and(i > 0, i < nm), i % 2 == parity))
        def _():
            project(h_refs[1 - parity])
            normalise(h_refs[parity])

    @pl.when(i == nm)
    def _():
        project(h_refs[(nm - 1) % 2])


def _inproj(x2, mod3, g, w, seq, tm, tn):
    t, d = x2.shape
    ncol = w.shape[1]
    nm, nn = t // tm, ncol // tn
    tiles_per_seq = seq // tm
    assert tm % nn == 0

    def x_tile(i):
        return jnp.minimum(i, nm - 1)

    def mod_spec(chunk):
        return pl.BlockSpec((1, 1, d), lambda i, j: (x_tile(i) // tiles_per_seq, 0, chunk))

    def w_tile(i, j):
        return jnp.where(i == 0, 0, j)

    return pl.pallas_call(
        functools.partial(_inproj_kernel, nm=nm, nn=nn),
        grid=(nm + 1, nn),
        in_specs=[
            pl.BlockSpec((tm, d), lambda i, j: (x_tile(i), 0)),
            mod_spec(0),
            mod_spec(1),
            pl.BlockSpec((1, d), lambda i, j: (0, 0)),
            pl.BlockSpec((d, tn), lambda i, j: (0, w_tile(i, j))),
        ],
        out_specs=pl.BlockSpec((tm, tn), lambda i, j: (jnp.maximum(i - 1, 0), w_tile(i, j))),
        out_shape=jax.ShapeDtypeStruct((t, ncol), BF16),
        scratch_shapes=[pltpu.VMEM((tm, d), BF16), pltpu.VMEM((tm, d), BF16)],
        compiler_params=_params(("arbitrary", "arbitrary")),
        name="inproj",
    )(x2, mod3, mod3, g, w)


def _rms_rows(x, g):
    inv = lax.rsqrt(jnp.mean(x * x, axis=-1, keepdims=True) + EPS)
    return x * inv * g


def _rotate_half_pairs(x):
    lane = lax.broadcasted_iota(jnp.int32, x.shape, 1)
    first = (lane % ROPE_DIM) < (ROPE_DIM // 2)
    return jnp.where(first, pltpu.roll(x, LANES - ROPE_DIM // 2, 1), pltpu.roll(x, ROPE_DIM // 2, 1))


def _qkv_kernel(qlat_ref, kvlat_ref, krope_ref, cos_ref, sin_ref, cost_ref, sint_ref, gql_ref, gkvl_ref,
                wqt_ref, wk_ref, wvt_ref, gq_ref, gkn_ref, gkr_ref, qt_ref, k_ref, vt_ref):
    qn = _rms_rows(qlat_ref[...].astype(F32), gql_ref[...]).astype(BF16)
    kvn = _rms_rows(kvlat_ref[...].astype(F32), gkvl_ref[...]).astype(BF16)
    qt = lax.dot_general(wqt_ref[...], qn, _NT, preferred_element_type=F32)
    kn = jnp.dot(kvn, wk_ref[...], preferred_element_type=F32)
    vt = lax.dot_general(wvt_ref[...], kvn, _NT, preferred_element_type=F32)

    cost = cost_ref[...]
    sint = sint_ref[...]
    gq = gq_ref[...]
    half = ROPE_DIM // 2
    q_scale = math.log2(math.e) / math.sqrt(QK_DIM)
    for h in range(N_HEADS):
        blk = qt[h * QK_DIM:(h + 1) * QK_DIM, :]
        inv = lax.rsqrt(jnp.sum(blk * blk, axis=0, keepdims=True) * (1.0 / QK_DIM) + EPS) * q_scale
        y = blk * gq
        x1 = y[NOPE_DIM:NOPE_DIM + half, :]
        x2 = y[NOPE_DIM + half:QK_DIM, :]
        qt_ref[0, h, 0:NOPE_DIM, :] = (y[0:NOPE_DIM, :] * inv).astype(BF16)
        qt_ref[0, h, NOPE_DIM:NOPE_DIM + half, :] = ((x1 * cost - x2 * sint) * inv).astype(BF16)
        qt_ref[0, h, NOPE_DIM + half:QK_DIM, :] = ((x2 * cost + x1 * sint) * inv).astype(BF16)

    kr = krope_ref[...].astype(F32)
    kr_ss = jnp.sum(kr * kr, axis=-1, keepdims=True)
    krg = kr * gkr_ref[...]
    kr_rot = krg * cos_ref[...] + _rotate_half_pairs(krg) * sin_ref[...]
    ones_rows = jnp.ones((VT_ROWS - V_DIM, vt.shape[1]), BF16)
    for h in range(N_HEADS):
        k_nope = kn[:, h * NOPE_DIM:(h + 1) * NOPE_DIM]
        ss = jnp.sum(k_nope * k_nope, axis=-1, keepdims=True) + kr_ss
        inv = lax.rsqrt(ss * (1.0 / QK_DIM) + EPS)
        k_ref[0, h, :, 0:NOPE_DIM] = (k_nope * inv * gkn_ref[...]).astype(BF16)
        k_ref[0, h, :, NOPE_DIM:QK_DIM] = (kr_rot[:, 0:ROPE_DIM] * inv).astype(BF16)
        vt_ref[0, h, 0:V_DIM, :] = vt[h * V_DIM:(h + 1) * V_DIM, :].astype(BF16)
        vt_ref[0, h, V_DIM:VT_ROWS, :] = ones_rows


def _qkv_up(z, cos_t, sin_t, cos_ft, sin_ft, gql, gkvl, wqt, wk, wvt, gq, gkn, gkr, batch, seq, tm):
    tiles_per_seq = seq // tm

    def zcol(width, offset):
        return pl.BlockSpec((tm, width), lambda i: (i, offset // width))

    def full(arr):
        return pl.BlockSpec(arr.shape, lambda i: (0,) * arr.ndim)

    def feat_major(rows):
        return pl.BlockSpec((1, N_HEADS, rows, tm), lambda i: (i // tiles_per_seq, 0, 0, i % tiles_per_seq))

    k_out = pl.BlockSpec((1, N_HEADS, tm, QK_DIM), lambda i: (i // tiles_per_seq, 0, i % tiles_per_seq, 0))
    row = pl.BlockSpec((tm, LANES), lambda i: (i, 0))
    col = pl.BlockSpec((ROPE_DIM // 2, tm), lambda i: (0, i))
    return pl.pallas_call(
        _qkv_kernel,
        grid=(batch * tiles_per_seq,),
        in_specs=[zcol(Q_LORA, Z_Q_LAT), zcol(KV_LORA, Z_KV_LAT), zcol(LANES, Z_K_ROPE), row, row, col, col,
                  full(gql), full(gkvl), full(wqt), full(wk), full(wvt), full(gq), full(gkn), full(gkr)],
        out_specs=[feat_major(QK_DIM), k_out, feat_major(VT_ROWS)],
        out_shape=[jax.ShapeDtypeStruct((batch, N_HEADS, QK_DIM, seq), BF16),
                   jax.ShapeDtypeStruct((batch, N_HEADS, seq, QK_DIM), BF16),
                   jax.ShapeDtypeStruct((batch, N_HEADS, VT_ROWS, seq), BF16)],
        compiler_params=_params(("parallel",)),
        name="qkv_up",
    )(z, z, z, cos_t, sin_t, cos_ft, sin_ft, gql, gkvl, wqt, wk, wvt, gq, gkn, gkr)


def _attn_kernel(qt_ref, k_ref, vt_ref, gate_ref, o_ref, m_ref, alpha_ref, acc_ref, p_ref, *, tq, qc, hp):
    tk = tq
    qi = pl.program_id(2)
    m_ref[...] = jnp.full(m_ref.shape, -jnp.inf, F32)
    acc_ref[...] = jnp.zeros(acc_ref.shape, F32)
    chains = tuple((h, c) for h in range(hp) for c in range(tq // qc))

    def qcols(c):
        return slice(c * qc, (c + 1) * qc)

    def keys_of(kb):
        return pl.ds(pl.multiple_of(kb * tk, tk), tk)

    def qk(k, h, c):
        return jnp.dot(k, qt_ref[0, h, :, qcols(c)], preferred_element_type=F32)

    def softmax(h, c, parts):
        m_prev = m_ref[h, :, qcols(c)]
        m_new = m_prev
        for _, st in parts:
            m_new = jnp.maximum(m_new, jnp.max(st, axis=0, keepdims=True))
        alpha_ref[h, :, qcols(c)] = jnp.exp2(m_prev - m_new)
        for r0, st in parts:
            p_ref[h, r0:r0 + st.shape[0], qcols(c)] = jnp.exp2(st - m_new).astype(BF16)
        m_ref[h, :, qcols(c)] = m_new

    def pv(kb):
        vts = [vt_ref[0, h, :, keys_of(kb)] for h in range(hp)]
        for h, c in chains:
            acc_ref[h, :, qcols(c)] = (alpha_ref[h, :, qcols(c)] * acc_ref[h, :, qcols(c)]
                                       + jnp.dot(vts[h], p_ref[h, :, qcols(c)], preferred_element_type=F32))

    n_full = qi
    d0 = pl.multiple_of(qi * tq, tq)
    sts = [qk(k_ref[0, h, pl.ds(d0, (c + 1) * qc), :], h, c) for h, c in chains]
    tri = (lax.broadcasted_iota(jnp.int32, (qc, qc), 0) <= lax.broadcasted_iota(jnp.int32, (qc, qc), 1))
    for (h, c), st in zip(chains, sts):
        parts = [(c * qc, jnp.where(tri, st[c * qc:, :], -jnp.inf))]
        if c:
            parts.insert(0, (0, st[:c * qc, :]))
        softmax(h, c, parts)
        if (c + 1) * qc < tk:
            p_ref[h, (c + 1) * qc:tk, qcols(c)] = jnp.zeros((tk - (c + 1) * qc, qc), BF16)

    def body(kb, carry):
        ks = [k_ref[0, h, keys_of(kb), :] for h in range(hp)]
        sts = [qk(ks[h], h, c) for h, c in chains]
        pv(jnp.where(kb == 0, n_full, kb - 1))
        for (h, c), st in zip(chains, sts):
            softmax(h, c, [(0, st)])
        return carry

    lax.fori_loop(0, n_full, body, 0)
    pv(jnp.where(n_full == 0, n_full, n_full - 1))

    for h in range(hp):
        acc = acc_ref[h]
        out_t = acc[0:V_DIM, :] / acc[V_DIM:V_DIM + 1, :]
        gate = gate_ref[:, h * V_DIM:(h + 1) * V_DIM].astype(F32)
        o_ref[:, h * V_DIM:(h + 1) * V_DIM] = (out_t.T * _silu(gate)).astype(BF16)


def _attention(qt, k, vt, z, tq, hp):
    batch, heads, _, seq = qt.shape
    nq = seq // tq
    gate_col0 = Z_MLA_GATE // (hp * V_DIM)
    return pl.pallas_call(
        functools.partial(_attn_kernel, tq=tq, qc=min(tq, 256), hp=hp),
        grid=(batch, heads // hp, nq),
        in_specs=[
            pl.BlockSpec((1, hp, QK_DIM, tq), lambda b, h, i: (b, h, 0, i)),
            pl.BlockSpec((1, hp, seq, QK_DIM), lambda b, h, i: (b, h, 0, 0)),
            pl.BlockSpec((1, hp, VT_ROWS, seq), lambda b, h, i: (b, h, 0, 0)),
            pl.BlockSpec((tq, hp * V_DIM), lambda b, h, i: (b * nq + i, gate_col0 + h)),
        ],
        out_specs=pl.BlockSpec((tq, hp * V_DIM), lambda b, h, i: (b * nq + i, h)),
        out_shape=jax.ShapeDtypeStruct((batch * seq, heads * V_DIM), BF16),
        scratch_shapes=[pltpu.VMEM((hp, 1, tq), F32), pltpu.VMEM((hp, 1, tq), F32), pltpu.VMEM((hp, VT_ROWS, tq), F32),
                        pltpu.VMEM((hp, tq, tq), BF16)],
        compiler_params=_params(("parallel", "parallel", "arbitrary")),
        name="mla_attention",
    )(qt, k, vt, z)


HALO = 32
SUBLANES = 8
PITCH = 100
PHASES = 16


def _conv_kernel(uval_ref, ugate_ref, cgate_ref, glub_ref, dww_ref, dwb_ref, lng_ref, lnb_ref, wpw_ref, bpw_ref,
                 o_ref, ubuf_ref, cbuf_ref, carry_ref):
    ts, c = uval_ref.shape
    seg = ts // SUBLANES
    n_slabs = c // LANES
    first = HALO - (CONV_K - 1)

    @pl.when(pl.program_id(1) == 0)
    def _():
        carry_ref[...] = jnp.zeros(carry_ref.shape, F32)

    a = uval_ref[...].astype(F32) + glub_ref[:, 0:c]
    g = ugate_ref[...].astype(F32) + glub_ref[:, c:2 * c]
    u = a * _sigmoid(g)
    for s in range(n_slabs):
        us = u[:, s * LANES:(s + 1) * LANES]
        ubuf_ref[s, first:HALO, :] = carry_ref[s, first:HALO, :]
        for k in range(SUBLANES):
            ubuf_ref[s, k * PITCH + HALO:k * PITCH + HALO + seg, :] = us[k * seg:(k + 1) * seg]
            if k + 1 < SUBLANES:
                ubuf_ref[s, (k + 1) * PITCH + first:(k + 1) * PITCH + HALO, :] = (
                    us[(k + 1) * seg - (HALO - first):(k + 1) * seg])
        carry_ref[s, first:HALO, :] = us[ts - (HALO - first):ts]

    def slab_body(s, carry):
        bias = dwb_ref[s]

        def group_body(gi, carry2):
            b0 = gi * PHASES
            accs = [None] * PHASES
            for r in range(CONV_K - 1 + PHASES):
                rows = ubuf_ref[s, pl.ds(b0 + first + r, SUBLANES, stride=PITCH), :]
                for ph in range(PHASES):
                    j = r - ph
                    if 0 <= j < CONV_K:
                        term = dww_ref[s, j:j + 1, :] * rows
                        accs[ph] = term if accs[ph] is None else accs[ph] + term
            for ph in range(PHASES):
                cbuf_ref[s, pl.ds(b0 + ph, SUBLANES, stride=seg), :] = accs[ph] + bias
            return carry2

        return lax.fori_loop(0, seg // PHASES, group_body, carry)

    lax.fori_loop(0, n_slabs, slab_body, 0)

    y = jnp.concatenate([cbuf_ref[s] for s in range(n_slabs)], axis=-1)
    mu = jnp.mean(y, axis=-1, keepdims=True)
    yc = y - mu
    var = jnp.mean(yc * yc, axis=-1, keepdims=True)
    yn = yc * lax.rsqrt(var + EPS) * lng_ref[...] + lnb_ref[...]
    act = _silu(yn).astype(BF16)
    pw = jnp.dot(act, wpw_ref[...], preferred_element_type=F32) + bpw_ref[...]
    o_ref[...] = (pw * _silu(cgate_ref[...].astype(F32))).astype(BF16)


def _conv_module(z, glub, dww, dwb, lng, lnb, wpw, bpw, batch, seq, ts):
    c = wpw.shape[0]
    n_slabs = c // LANES
    tiles_per_seq = seq // ts
    assert ts % (SUBLANES * PHASES) == 0 and HALO + ts // SUBLANES <= PITCH

    def zcol(offset):
        return pl.BlockSpec((ts, c), lambda b, i: (b * tiles_per_seq + i, offset // c))

    def full(arr):
        return pl.BlockSpec(arr.shape, lambda b, i: (0,) * arr.ndim)

    return pl.pallas_call(
        _conv_kernel,
        grid=(batch, tiles_per_seq),
        in_specs=[zcol(Z_CONV_IN), zcol(Z_CONV_IN + c), zcol(Z_CONV_GATE),
                  full(glub), full(dww), full(dwb), full(lng), full(lnb), full(wpw), full(bpw)],
        out_specs=pl.BlockSpec((ts, c), lambda b, i: (b * tiles_per_seq + i, 0)),
        out_shape=jax.ShapeDtypeStruct((batch * seq, c), BF16),
        scratch_shapes=[pltpu.VMEM((n_slabs, SUBLANES * PITCH, LANES), F32), pltpu.VMEM((n_slabs, ts, LANES), F32),
                        pltpu.VMEM((n_slabs, HALO, LANES), F32)],
        compiler_params=_params(("parallel", "arbitrary")),
        name="conv_module",
    )(z, z, z, glub, dww, dwb, lng, lnb, wpw, bpw)


def _slabs(w):
    rows, c = w.shape
    return w.reshape(rows, c // LANES, LANES).transpose(1, 0, 2)


def _outproj_kernel(mla_ref, conv_ref, x_ref, gate_ref, wa_ref, wb_ref, o_ref):
    y = jnp.dot(mla_ref[...], wa_ref[...], preferred_element_type=F32)
    y = y + jnp.dot(conv_ref[...], wb_ref[...], preferred_element_type=F32)
    o_ref[...] = x_ref[...] + gate_ref[0] * y


def _outproj(mla, conv, x2, mod3, w_a, w_b, seq, tm):
    t, d = x2.shape
    half = mla.shape[1]
    tiles_per_seq = seq // tm
    return pl.pallas_call(
        _outproj_kernel,
        grid=(t // tm,),
        in_specs=[
            pl.BlockSpec((tm, half), lambda i: (i, 0)),
            pl.BlockSpec((tm, half), lambda i: (i, 0)),
            pl.BlockSpec((tm, d), lambda i: (i, 0)),
            pl.BlockSpec((1, 1, d), lambda i: (i // tiles_per_seq, 0, 2)),
            pl.BlockSpec((half, d), lambda i: (0, 0)),
            pl.BlockSpec((half, d), lambda i: (0, 0)),
        ],
        out_specs=pl.BlockSpec((tm, d), lambda i: (i, 0)),
        out_shape=jax.ShapeDtypeStruct((t, d), F32),
        compiler_params=_params(("parallel",)),
        name="outproj",
    )(mla, conv, x2, mod3, w_a, w_b)


def _tile(n, want):
    t = min(n, want)
    assert n % t == 0, (n, t)
    return t


def kernel(x, c, positions, ada_w, ada_b, norm_g, w_in, q_lat_g, w_q_up, kv_lat_g, w_kv_up, q_norm_g, k_norm_g,
           glu_b, dw_w, dw_b, conv_ln_g, conv_ln_b, w_pw, b_pw, w_out):
    batch, seq, d = x.shape
    depth = ada_w.shape[0]
    t = batch * seq
    d_mla = N_HEADS * V_DIM
    d_conv = w_pw.shape[1]
    assert w_in.shape[2] == Q_LORA + KV_LORA + ROPE_DIM + d_mla + 3 * d_conv
    assert d_mla == 1024 and d_conv == 1024 and d == 2048

    inv_freq = 1.0 / (ROPE_THETA ** (jnp.arange(0, ROPE_DIM, 2, dtype=F32) / ROPE_DIM))
    inv128 = jnp.tile(inv_freq, LANES // (ROPE_DIM // 2))[None, :]
    sign64 = jnp.concatenate([-jnp.ones((ROPE_DIM // 2,), F32), jnp.ones((ROPE_DIM // 2,), F32)])
    sign128 = jnp.tile(sign64, LANES // ROPE_DIM)[None, :]
    pos_b = jnp.broadcast_to(positions.reshape(t, 1), (t, LANES))
    pos_t = jnp.broadcast_to(positions.reshape(1, t), (ROPE_DIM // 2, t))
    inv_t = jnp.broadcast_to(inv_freq[:, None], (ROPE_DIM // 2, _tile(t, 2048)))
    cos_t, sin_t, cos_ft, sin_ft = _rope_tables(pos_b, inv128, sign128, pos_t, inv_t)

    c_rows = 8
    c_pad = jnp.zeros((c_rows, d), F32).at[:batch].set(c)
    mod = _adaln(c_pad, ada_w, ada_b.reshape(depth, 1, 3 * d), _tile(3 * d, 1024))

    x2 = x.reshape(t, d)
    s0 = Q_LORA
    s1 = s0 + KV_LORA
    s2 = s1 + ROPE_DIM
    s3 = s2 + d_mla
    s4 = s3 + 2 * d_conv
    for l in range(depth):
        mod3 = mod[l].reshape(c_rows, 1, 3 * d)
        wl = w_in[l]
        w_z = jnp.concatenate(
            [wl[:, a:b].astype(BF16) for a, b in ((s3, s4), (s2, s3), (s4, wl.shape[1]), (0, s0), (s0, s1), (s1, s2))]
            + [jnp.zeros((d, Z_COLS - wl.shape[1]), BF16)], axis=1)
        wqt = w_q_up[l].T.astype(BF16)
        wkv = w_kv_up[l].reshape(KV_LORA, N_HEADS, NOPE_DIM + V_DIM)
        wk = wkv[:, :, :NOPE_DIM].reshape(KV_LORA, N_HEADS * NOPE_DIM).astype(BF16)
        wvt = wkv[:, :, NOPE_DIM:].reshape(KV_LORA, N_HEADS * V_DIM).T.astype(BF16)
        gq, gk = q_norm_g[l], k_norm_g[l]
        gkn = gk[None, :NOPE_DIM]
        gkr = jnp.concatenate([gk[NOPE_DIM:], jnp.zeros((LANES - ROPE_DIM,), F32)])[None, :]

        z = _inproj(x2, mod3, norm_g[l][None, :], w_z, seq, _tile(seq, 1024), 1280)
        tm_qkv = _tile(seq, 512)
        qt, k, vt = _qkv_up(z, cos_t, sin_t, cos_ft, sin_ft, q_lat_g[l][None, :], kv_lat_g[l][None, :], wqt, wk, wvt,
                            jnp.broadcast_to(gq[:, None], (QK_DIM, tm_qkv)), gkn, gkr, batch, seq, tm_qkv)
        mla = _attention(qt, k, vt, z, _tile(seq, 1024), 4)
        dww = jnp.concatenate([dw_w[l], jnp.zeros((HALO - CONV_K, d_conv), F32)], axis=0)
        conv = _conv_module(z, glu_b[l][None, :], _slabs(dww), _slabs(dw_b[l][None, :]), conv_ln_g[l][None, :],
                            conv_ln_b[l][None, :], w_pw[l].astype(BF16), b_pw[l][None, :], batch, seq,
                            _tile(seq, 512))
        wo = w_out[l].astype(BF16)
        x2 = _outproj(mla, conv, x2, mod3, wo[:d_mla], wo[d_mla:], seq, _tile(seq, 512))
    return x2.reshape(batch, seq, d)
```

```python
import functools
import math

import jax
import jax.numpy as jnp
from jax import lax
from jax.experimental import pallas as pl
from jax.experimental.pallas import tpu as pltpu

F32 = jnp.float32
BF16 = jnp.bfloat16

N_HEADS = 8
NOPE_DIM = 128
ROPE_DIM = 64
V_DIM = 128
QK_DIM = NOPE_DIM + ROPE_DIM
Q_LORA = 512
KV_LORA = 256
ROPE_THETA = 10000.0
CONV_K = 31
EPS = 1e-6
LANES = 128
BF16_SUBLANES = 16
VT_ROWS = V_DIM + BF16_SUBLANES
_NT = (((1,), (1,)), ((), ()))

Z_CONV_IN = 0
Z_MLA_GATE = 2048
Z_CONV_GATE = 3072
Z_Q_LAT = 4096
Z_KV_LAT = 4608
Z_K_ROPE = 4864
Z_COLS = 5120

VMEM_LIMIT = 56 * 1024 * 1024


def _sigmoid(x):
    return 1.0 / (1.0 + jnp.exp(-x))


def _silu(x):
    return x * _sigmoid(x)


def _params(sem, vmem=VMEM_LIMIT):
    return pltpu.CompilerParams(dimension_semantics=sem, vmem_limit_bytes=vmem)


def _rope_table_kernel(pos_ref, inv_ref, cos_ref, sin_ref):
    ang = pos_ref[...].astype(F32) * inv_ref[...]
    cos_ref[...] = jnp.cos(ang)
    sin_ref[...] = jnp.sin(ang)


def _rope_tables(pos_t, inv_t):
    half, t = pos_t.shape
    tr = inv_t.shape[1]
    col = pl.BlockSpec((half, tr), lambda i: (0, i))
    return pl.pallas_call(
        _rope_table_kernel,
        grid=(t // tr,),
        in_specs=[col, pl.BlockSpec((half, tr), lambda i: (0, 0))],
        out_specs=[col, col],
        out_shape=[jax.ShapeDtypeStruct((half, t), F32)] * 2,
        compiler_params=_params(("parallel",)),
        name="rope_tables",
    )(pos_t, inv_t)


def _adaln_kernel(c_ref, w_ref, b_ref, o_ref):
    c_act = _silu(c_ref[...]).astype(BF16)
    acc = jnp.dot(c_act, w_ref[0].astype(BF16), preferred_element_type=F32)
    o_ref[0] = acc + b_ref[0]


def _adaln(c_pad, ada_w, ada_b3, tn):
    depth, d, n = ada_w.shape
    rows = c_pad.shape[0]
    return pl.pallas_call(
        _adaln_kernel,
        grid=(depth, n // tn),
        in_specs=[
            pl.BlockSpec((rows, d), lambda l, j: (0, 0)),
            pl.BlockSpec((1, d, tn), lambda l, j: (l, 0, j)),
            pl.BlockSpec((1, 1, tn), lambda l, j: (l, 0, j)),
        ],
        out_specs=pl.BlockSpec((1, rows, tn), lambda l, j: (l, 0, j)),
        out_shape=jax.ShapeDtypeStruct((depth, rows, n), F32),
        compiler_params=_params(("parallel", "parallel")),
        name="adaln_mod",
    )(c_pad, ada_w, ada_b3)


def _wprep_kernel(w_ref, o_ref, *, segments):
    col = 0
    for a, b in segments:
        o_ref[0, :, col:col + (b - a)] = w_ref[0, :, a:b].astype(BF16)
        col += b - a
    o_ref[0, :, col:] = jnp.zeros((o_ref.shape[1], o_ref.shape[2] - col), BF16)


def _wprep(w_in, segments, rb):
    depth, d, n = w_in.shape
    return pl.pallas_call(
        functools.partial(_wprep_kernel, segments=segments),
        grid=(depth, d // rb),
        in_specs=[pl.BlockSpec((1, rb, n), lambda l, i: (l, i, 0))],
        out_specs=pl.BlockSpec((1, rb, Z_COLS), lambda l, i: (l, i, 0)),
        out_shape=jax.ShapeDtypeStruct((depth, d, Z_COLS), BF16),
        compiler_params=_params(("parallel", "parallel")),
        name="w_in_layout",
    )(w_in)


def _inproj_kernel(x_ref, shift_ref, scale_ref, g_ref, w_ref, z_ref, h0_ref, h1_ref, *, nm, nn):
    i, j = pl.program_id(0), pl.program_id(1)
    tm, d = x_ref.shape
    rows_per_step = tm // nn
    h_refs = (h0_ref, h1_ref)

    def normalise(h_ref):
        rows = pl.ds(pl.multiple_of(j * rows_per_step, rows_per_step), rows_per_step)
        x = x_ref[rows, :]
        inv = lax.rsqrt(jnp.sum(x * x, axis=-1, keepdims=True) * (1.0 / d) + EPS)
        h_ref[rows, :] = (x * inv * (g_ref[...] * (1.0 + scale_ref[0])) + shift_ref[0]).astype(BF16)

    def project(h_ref):
        z_ref[...] = jnp.dot(h_ref[...], w_ref[...], preferred_element_type=F32).astype(BF16)

    @pl.when(i == 0)
    def _():
        normalise(h_refs[0])

    for parity in range(2):
        @pl.when(jnp.logical_and(jnp.logical_and(i > 0, i < nm), i % 2 == parity))
        def _():
            project(h_refs[1 - parity])
            normalise(h_refs[parity])

    @pl.when(i == nm)
    def _():
        project(h_refs[(nm - 1) % 2])


def _inproj(x2, mod3, g, w, seq, tm, tn):
    t, d = x2.shape
    ncol = w.shape[1]
    nm, nn = t // tm, ncol // tn
    tiles_per_seq = seq // tm
    assert tm % nn == 0

    def x_tile(i):
        return jnp.minimum(i, nm - 1)

    def mod_spec(chunk):
        return pl.BlockSpec((1, 1, d), lambda i, j: (x_tile(i) // tiles_per_seq, 0, chunk))

    def w_tile(i, j):
        return jnp.where(i == 0, 0, j)

    return pl.pallas_call(
        functools.partial(_inproj_kernel, nm=nm, nn=nn),
        grid=(nm + 1, nn),
        in_specs=[
            pl.BlockSpec((tm, d), lambda i, j: (x_tile(i), 0)),
            mod_spec(0),
            mod_spec(1),
            pl.BlockSpec((1, d), lambda i, j: (0, 0)),
            pl.BlockSpec((d, tn), lambda i, j: (0, w_tile(i, j))),
        ],
        out_specs=pl.BlockSpec((tm, tn), lambda i, j: (jnp.maximum(i - 1, 0), w_tile(i, j))),
        out_shape=jax.ShapeDtypeStruct((t, ncol), BF16),
        scratch_shapes=[pltpu.VMEM((tm, d), BF16), pltpu.VMEM((tm, d), BF16)],
        compiler_params=_params(("arbitrary", "arbitrary")),
        name="inproj",
    )(x2, mod3, mod3, g, w)


def _rms_rows(x, g):
    inv = lax.rsqrt(jnp.mean(x * x, axis=-1, keepdims=True) + EPS)
    return x * inv * g


def _rotate_half_pairs(x):
    lane = lax.broadcasted_iota(jnp.int32, x.shape, 1)
    first = (lane % ROPE_DIM) < (ROPE_DIM // 2)
    return jnp.where(first, pltpu.roll(x, LANES - ROPE_DIM // 2, 1), pltpu.roll(x, ROPE_DIM // 2, 1))


def _qkv_kernel(qlat_ref, kvlat_ref, krope_ref, cost_ref, sint_ref, gql_ref, gkvl_ref,
                wqt_ref, wk_ref, wvt_ref, gq_ref, gkn_ref, gkr_ref, qt_ref, k_ref, vt_ref):
    qn = _rms_rows(qlat_ref[...].astype(F32), gql_ref[...]).astype(BF16)
    kvn = _rms_rows(kvlat_ref[...].astype(F32), gkvl_ref[...]).astype(BF16)
    qt = lax.dot_general(wqt_ref[...], qn, _NT, preferred_element_type=F32)
    kn = jnp.dot(kvn, wk_ref[...], preferred_element_type=F32)
    vt = lax.dot_general(wvt_ref[...], kvn, _NT, preferred_element_type=F32)

    cost = cost_ref[...]
    sint = sint_ref[...]
    gq = gq_ref[...]
    half = ROPE_DIM // 2
    q_scale = math.log2(math.e) / math.sqrt(QK_DIM)
    for h in range(N_HEADS):
        blk = qt[h * QK_DIM:(h + 1) * QK_DIM, :]
        inv = lax.rsqrt(jnp.sum(blk * blk, axis=0, keepdims=True) * (1.0 / QK_DIM) + EPS) * q_scale
        y = blk * gq
        x1 = y[NOPE_DIM:NOPE_DIM + half, :]
        x2 = y[NOPE_DIM + half:QK_DIM, :]
        qt_ref[0, h, 0:NOPE_DIM, :] = (y[0:NOPE_DIM, :] * inv).astype(BF16)
        qt_ref[0, h, NOPE_DIM:NOPE_DIM + half, :] = ((x1 * cost - x2 * sint) * inv).astype(BF16)
        qt_ref[0, h, NOPE_DIM + half:QK_DIM, :] = ((x2 * cost + x1 * sint) * inv).astype(BF16)

    kr = krope_ref[...].astype(F32)
    kr_ss = jnp.sum(kr * kr, axis=-1, keepdims=True)
    krg = kr * gkr_ref[...]
    cos_tok = jnp.concatenate([cost] * (LANES // half), axis=0).T
    sin_tok = jnp.concatenate([-sint, sint] * (LANES // ROPE_DIM), axis=0).T
    kr_rot = krg * cos_tok + _rotate_half_pairs(krg) * sin_tok
    ones_rows = jnp.ones((VT_ROWS - V_DIM, vt.shape[1]), BF16)
    for h in range(N_HEADS):
        k_nope = kn[:, h * NOPE_DIM:(h + 1) * NOPE_DIM]
        ss = jnp.sum(k_nope * k_nope, axis=-1, keepdims=True) + kr_ss
        inv = lax.rsqrt(ss * (1.0 / QK_DIM) + EPS)
        k_ref[0, h, :, 0:NOPE_DIM] = (k_nope * inv * gkn_ref[...]).astype(BF16)
        k_ref[0, h, :, NOPE_DIM:QK_DIM] = (kr_rot[:, 0:ROPE_DIM] * inv).astype(BF16)
        vt_ref[0, h, 0:V_DIM, :] = vt[h * V_DIM:(h + 1) * V_DIM, :].astype(BF16)
        vt_ref[0, h, V_DIM:VT_ROWS, :] = ones_rows


def _qkv_up(z, cos_ft, sin_ft, gql, gkvl, wqt, wk, wvt, gq, gkn, gkr, batch, seq, tm):
    tiles_per_seq = seq // tm

    def zcol(width, offset):
        return pl.BlockSpec((tm, width), lambda i: (i, offset // width))

    def full(arr):
        return pl.BlockSpec(arr.shape, lambda i: (0,) * arr.ndim)

    def feat_major(rows):
        return pl.BlockSpec((1, N_HEADS, rows, tm), lambda i: (i // tiles_per_seq, 0, 0, i % tiles_per_seq))

    k_out = pl.BlockSpec((1, N_HEADS, tm, QK_DIM), lambda i: (i // tiles_per_seq, 0, i % tiles_per_seq, 0))
    col = pl.BlockSpec((ROPE_DIM // 2, tm), lambda i: (0, i))
    return pl.pallas_call(
        _qkv_kernel,
        grid=(batch * tiles_per_seq,),
        in_specs=[zcol(Q_LORA, Z_Q_LAT), zcol(KV_LORA, Z_KV_LAT), zcol(LANES, Z_K_ROPE), col, col,
                  full(gql), full(gkvl), full(wqt), full(wk), full(wvt), full(gq), full(gkn), full(gkr)],
        out_specs=[feat_major(QK_DIM), k_out, feat_major(VT_ROWS)],
        out_shape=[jax.ShapeDtypeStruct((batch, N_HEADS, QK_DIM, seq), BF16),
                   jax.ShapeDtypeStruct((batch, N_HEADS, seq, QK_DIM), BF16),
                   jax.ShapeDtypeStruct((batch, N_HEADS, VT_ROWS, seq), BF16)],
        compiler_params=_params(("parallel",)),
        name="qkv_up",
    )(z, z, z, cos_ft, sin_ft, gql, gkvl, wqt, wk, wvt, gq, gkn, gkr)


def _attn_kernel(qt_ref, k_ref, vt_ref, gate_ref, o_ref, m_ref, alpha_ref, acc_ref, p_ref, *, tq, qc, hp):
    tk = tq
    qi = pl.program_id(2)
    m_ref[...] = jnp.full(m_ref.shape, -jnp.inf, F32)
    acc_ref[...] = jnp.zeros(acc_ref.shape, F32)
    chains = tuple((h, c) for h in range(hp) for c in range(tq // qc))

    def qcols(c):
        return slice(c * qc, (c + 1) * qc)

    def keys_of(kb):
        return pl.ds(pl.multiple_of(kb * tk, tk), tk)

    def qk(k, h, c):
        return jnp.dot(k, qt_ref[0, h, :, qcols(c)], preferred_element_type=F32)

    def softmax(h, c, parts):
        m_prev = m_ref[h, :, qcols(c)]
        m_new = m_prev
        for _, st in parts:
            m_new = jnp.maximum(m_new, jnp.max(st, axis=0, keepdims=True))
        alpha_ref[h, :, qcols(c)] = jnp.exp2(m_prev - m_new)
        for r0, st in parts:
            p_ref[h, r0:r0 + st.shape[0], qcols(c)] = jnp.exp2(st - m_new).astype(BF16)
        m_ref[h, :, qcols(c)] = m_new

    def pv(kb):
        vts = [vt_ref[0, h, :, keys_of(kb)] for h in range(hp)]
        for h, c in chains:
            acc_ref[h, :, qcols(c)] = (alpha_ref[h, :, qcols(c)] * acc_ref[h, :, qcols(c)]
                                       + jnp.dot(vts[h], p_ref[h, :, qcols(c)], preferred_element_type=F32))

    n_full = qi
    d0 = pl.multiple_of(qi * tq, tq)
    sts = [qk(k_ref[0, h, pl.ds(d0, (c + 1) * qc), :], h, c) for h, c in chains]
    tri = (lax.broadcasted_iota(jnp.int32, (qc, qc), 0) <= lax.broadcasted_iota(jnp.int32, (qc, qc), 1))
    for (h, c), st in zip(chains, sts):
        parts = [(c * qc, jnp.where(tri, st[c * qc:, :], -jnp.inf))]
        if c:
            parts.insert(0, (0, st[:c * qc, :]))
        softmax(h, c, parts)
        if (c + 1) * qc < tk:
            p_ref[h, (c + 1) * qc:tk, qcols(c)] = jnp.zeros((tk - (c + 1) * qc, qc), BF16)

    def body(kb, carry):
        ks = [k_ref[0, h, keys_of(kb), :] for h in range(hp)]
        sts = [qk(ks[h], h, c) for h, c in chains]
        pv(jnp.where(kb == 0, n_full, kb - 1))
        for (h, c), st in zip(chains, sts):
            softmax(h, c, [(0, st)])
        return carry

    lax.fori_loop(0, n_full, body, 0)
    pv(jnp.where(n_full == 0, n_full, n_full - 1))

    for h in range(hp):
        acc = acc_ref[h]
        out_t = acc[0:V_DIM, :] / acc[V_DIM:V_DIM + 1, :]
        gate = gate_ref[:, h * V_DIM:(h + 1) * V_DIM].astype(F32)
        o_ref[:, h * V_DIM:(h + 1) * V_DIM] = (out_t.T * _silu(gate)).astype(BF16)


def _attention(qt, k, vt, z, tq, hp):
    batch, heads, _, seq = qt.shape
    nq = seq // tq
    gate_col0 = Z_MLA_GATE // (hp * V_DIM)
    return pl.pallas_call(
        functools.partial(_attn_kernel, tq=tq, qc=min(tq, 256), hp=hp),
        grid=(batch, heads // hp, nq),
        in_specs=[
            pl.BlockSpec((1, hp, QK_DIM, tq), lambda b, h, i: (b, h, 0, i)),
            pl.BlockSpec((1, hp, seq, QK_DIM), lambda b, h, i: (b, h, 0, 0)),
            pl.BlockSpec((1, hp, VT_ROWS, seq), lambda b, h, i: (b, h, 0, 0)),
            pl.BlockSpec((tq, hp * V_DIM), lambda b, h, i: (b * nq + i, gate_col0 + h)),
        ],
        out_specs=pl.BlockSpec((tq, hp * V_DIM), lambda b, h, i: (b * nq + i, h)),
        out_shape=jax.ShapeDtypeStruct((batch * seq, heads * V_DIM), BF16),
        scratch_shapes=[pltpu.VMEM((hp, 1, tq), F32), pltpu.VMEM((hp, 1, tq), F32), pltpu.VMEM((hp, VT_ROWS, tq), F32),
                        pltpu.VMEM((hp, tq, tq), BF16)],
        compiler_params=_params(("parallel", "parallel", "arbitrary")),
        name="mla_attention",
    )(qt, k, vt, z)


HALO = 32
SUBLANES = 8
PITCH = 100
PHASES = 8


def _conv_kernel(uval_ref, ugate_ref, cgate_ref, glub_ref, dww_ref, dwb_ref, lng_ref, lnb_ref, wpw_ref, bpw_ref,
                 o_ref, ubuf_ref, cbuf_ref, carry_ref):
    ts, c = uval_ref.shape
    seg = ts // SUBLANES
    n_slabs = c // LANES
    first = HALO - (CONV_K - 1)

    @pl.when(pl.program_id(1) == 0)
    def _():
        carry_ref[...] = jnp.zeros(carry_ref.shape, F32)

    a = uval_ref[...].astype(F32) + glub_ref[:, 0:c]
    g = ugate_ref[...].astype(F32) + glub_ref[:, c:2 * c]
    u = a * _sigmoid(g)
    for s in range(n_slabs):
        us = u[:, s * LANES:(s + 1) * LANES]
        ubuf_ref[s, first:HALO, :] = carry_ref[s, first:HALO, :]
        for k in range(SUBLANES):
            ubuf_ref[s, k * PITCH + HALO:k * PITCH + HALO + seg, :] = us[k * seg:(k + 1) * seg]
            if k + 1 < SUBLANES:
                ubuf_ref[s, (k + 1) * PITCH + first:(k + 1) * PITCH + HALO, :] = (
                    us[(k + 1) * seg - (HALO - first):(k + 1) * seg])
        carry_ref[s, first:HALO, :] = us[ts - (HALO - first):ts]

    def slab_body(s, carry):
        bias = dwb_ref[s]

        def group_body(gi, carry2):
            b0 = gi * PHASES
            accs = [None] * PHASES
            for r in range(CONV_K - 1 + PHASES):
                rows = ubuf_ref[s, pl.ds(b0 + first + r, SUBLANES, stride=PITCH), :]
                for ph in range(PHASES):
                    j = r - ph
                    if 0 <= j < CONV_K:
                        term = dww_ref[s, j:j + 1, :] * rows
                        accs[ph] = term if accs[ph] is None else accs[ph] + term
            for ph in range(PHASES):
                cbuf_ref[s, pl.ds(b0 + ph, SUBLANES, stride=seg), :] = accs[ph] + bias
            return carry2

        return lax.fori_loop(0, seg // PHASES, group_body, carry, unroll=True)

    lax.fori_loop(0, n_slabs, slab_body, 0)

    y = jnp.concatenate([cbuf_ref[s] for s in range(n_slabs)], axis=-1)
    mu = jnp.mean(y, axis=-1, keepdims=True)
    yc = y - mu
    var = jnp.mean(yc * yc, axis=-1, keepdims=True)
    yn = yc * lax.rsqrt(var + EPS) * lng_ref[...] + lnb_ref[...]
    act = _silu(yn).astype(BF16)
    pw = jnp.dot(act, wpw_ref[...], preferred_element_type=F32) + bpw_ref[...]
    o_ref[...] = (pw * _silu(cgate_ref[...].astype(F32))).astype(BF16)


def _conv_module(z, glub, dww, dwb, lng, lnb, wpw, bpw, batch, seq, ts):
    c = wpw.shape[0]
    n_slabs = c // LANES
    tiles_per_seq = seq // ts
    assert ts % (SUBLANES * PHASES) == 0 and HALO + ts // SUBLANES <= PITCH

    def zcol(offset):
        return pl.BlockSpec((ts, c), lambda b, i: (b * tiles_per_seq + i, offset // c))

    def full(arr):
        return pl.BlockSpec(arr.shape, lambda b, i: (0,) * arr.ndim)

    return pl.pallas_call(
        _conv_kernel,
        grid=(batch, tiles_per_seq),
        in_specs=[zcol(Z_CONV_IN), zcol(Z_CONV_IN + c), zcol(Z_CONV_GATE),
                  full(glub), full(dww), full(dwb), full(lng), full(lnb), full(wpw), full(bpw)],
        out_specs=pl.BlockSpec((ts, c), lambda b, i: (b * tiles_per_seq + i, 0)),
        out_shape=jax.ShapeDtypeStruct((batch * seq, c), BF16),
        scratch_shapes=[pltpu.VMEM((n_slabs, SUBLANES * PITCH, LANES), F32), pltpu.VMEM((n_slabs, ts, LANES), F32),
                        pltpu.VMEM((n_slabs, HALO, LANES), F32)],
        compiler_params=_params(("parallel", "arbitrary")),
        name="conv_module",
    )(z, z, z, glub, dww, dwb, lng, lnb, wpw, bpw)


def _slabs(w):
    rows, c = w.shape
    return w.reshape(rows, c // LANES, LANES).transpose(1, 0, 2)


def _outproj_kernel(mla_ref, conv_ref, x_ref, gate_ref, wa_ref, wb_ref, o_ref):
    y = jnp.dot(mla_ref[...], wa_ref[...], preferred_element_type=F32)
    y = y + jnp.dot(conv_ref[...], wb_ref[...], preferred_element_type=F32)
    o_ref[...] = x_ref[...] + gate_ref[0] * y


def _outproj(mla, conv, x2, mod3, w_a, w_b, seq, tm):
    t, d = x2.shape
    half = mla.shape[1]
    tiles_per_seq = seq // tm
    return pl.pallas_call(
        _outproj_kernel,
        grid=(t // tm,),
        in_specs=[
            pl.BlockSpec((tm, half), lambda i: (i, 0)),
            pl.BlockSpec((tm, half), lambda i: (i, 0)),
            pl.BlockSpec((tm, d), lambda i: (i, 0)),
            pl.BlockSpec((1, 1, d), lambda i: (i // tiles_per_seq, 0, 2)),
            pl.BlockSpec((half, d), lambda i: (0, 0)),
            pl.BlockSpec((half, d), lambda i: (0, 0)),
        ],
        out_specs=pl.BlockSpec((tm, d), lambda i: (i, 0)),
        out_shape=jax.ShapeDtypeStruct((t, d), F32),
        compiler_params=_params(("parallel",)),
        name="outproj",
    )(mla, conv, x2, mod3, w_a, w_b)


def _tile(n, want):
    t = min(n, want)
    assert n % t == 0, (n, t)
    return t


def kernel(x, c, positions, ada_w, ada_b, norm_g, w_in, q_lat_g, w_q_up, kv_lat_g, w_kv_up, q_norm_g, k_norm_g,
           glu_b, dw_w, dw_b, conv_ln_g, conv_ln_b, w_pw, b_pw, w_out):
    batch, seq, d = x.shape
    depth = ada_w.shape[0]
    t = batch * seq
    d_mla = N_HEADS * V_DIM
    d_conv = w_pw.shape[1]
    assert w_in.shape[2] == Q_LORA + KV_LORA + ROPE_DIM + d_mla + 3 * d_conv
    assert d_mla == 1024 and d_conv == 1024 and d == 2048

    inv_freq = 1.0 / (ROPE_THETA ** (jnp.arange(0, ROPE_DIM, 2, dtype=F32) / ROPE_DIM))
    pos_t = jnp.broadcast_to(positions.reshape(1, t), (ROPE_DIM // 2, t))
    inv_t = jnp.broadcast_to(inv_freq[:, None], (ROPE_DIM // 2, _tile(t, 2048)))
    cos_ft, sin_ft = _rope_tables(pos_t, inv_t)

    c_rows = 8
    c_pad = jnp.zeros((c_rows, d), F32).at[:batch].set(c)
    mod = _adaln(c_pad, ada_w, ada_b.reshape(depth, 1, 3 * d), _tile(3 * d, 1024))

    x2 = x.reshape(t, d)
    s0 = Q_LORA
    s1 = s0 + KV_LORA
    s2 = s1 + ROPE_DIM
    s3 = s2 + d_mla
    s4 = s3 + 2 * d_conv
    w_z = _wprep(w_in, ((s3, s4), (s2, s3), (s4, w_in.shape[2]), (0, s0), (s0, s1), (s1, s2)), 256)
    for l in range(depth):
        mod3 = mod[l].reshape(c_rows, 1, 3 * d)
        wqt = w_q_up[l].T.astype(BF16)
        wkv = w_kv_up[l].reshape(KV_LORA, N_HEADS, NOPE_DIM + V_DIM)
        wk = wkv[:, :, :NOPE_DIM].reshape(KV_LORA, N_HEADS * NOPE_DIM).astype(BF16)
        wvt = wkv[:, :, NOPE_DIM:].reshape(KV_LORA, N_HEADS * V_DIM).T.astype(BF16)
        gq, gk = q_norm_g[l], k_norm_g[l]
        gkn = gk[None, :NOPE_DIM]
        gkr = jnp.concatenate([gk[NOPE_DIM:], jnp.zeros((LANES - ROPE_DIM,), F32)])[None, :]

        z = _inproj(x2, mod3, norm_g[l][None, :], w_z[l], seq, _tile(seq, 1024), 1280)
        tm_qkv = _tile(seq, 512)
        qt, k, vt = _qkv_up(z, cos_ft, sin_ft, q_lat_g[l][None, :], kv_lat_g[l][None, :], wqt, wk, wvt,
                            jnp.broadcast_to(gq[:, None], (QK_DIM, tm_qkv)), gkn, gkr, batch, seq, tm_qkv)
        mla = _attention(qt, k, vt, z, _tile(seq, 1024), 4)
        dww = jnp.concatenate([dw_w[l], jnp.zeros((HALO - CONV_K, d_conv), F32)], axis=0)
        conv = _conv_module(z, glu_b[l][None, :], _slabs(dww), _slabs(dw_b[l][None, :]), conv_ln_g[l][None, :],
                            conv_ln_b[l][None, :], w_pw[l].astype(BF16), b_pw[l][None, :], batch, seq,
                            _tile(seq, 512))
        wo = w_out[l].astype(BF16)
        x2 = _outproj(mla, conv, x2, mod3, wo[:d_mla], wo[d_mla:], seq, _tile(seq, 512))
    return x2.reshape(batch, seq, d)
```

```python
import functools
import math

import jax
import jax.numpy as jnp
from jax import lax
from jax.experimental import pallas as pl
from jax.experimental.pallas import tpu as pltpu

F32 = jnp.float32
BF16 = jnp.bfloat16

N_HEADS = 8
NOPE_DIM = 128
ROPE_DIM = 64
V_DIM = 128
QK_DIM = NOPE_DIM + ROPE_DIM
Q_LORA = 512
KV_LORA = 256
ROPE_THETA = 10000.0
CONV_K = 31
EPS = 1e-6
LANES = 128
BF16_SUBLANES = 16
VT_ROWS = V_DIM + BF16_SUBLANES
_NT = (((1,), (1,)), ((), ()))

Z_CONV_IN = 0
Z_MLA_GATE = 2048
Z_CONV_GATE = 3072
Z_Q_LAT = 4096
Z_KV_LAT = 4608
Z_K_ROPE = 4864
Z_COLS = 5120

VMEM_LIMIT = 56 * 1024 * 1024


def _sigmoid(x):
    return 1.0 / (1.0 + jnp.exp(-x))


def _silu(x):
    return x * _sigmoid(x)


def _params(sem, vmem=VMEM_LIMIT):
    return pltpu.CompilerParams(dimension_semantics=sem, vmem_limit_bytes=vmem)


def _rope_table_kernel(pos_ref, inv_ref, cos_ref, sin_ref):
    ang = pos_ref[...].astype(F32) * inv_ref[...]
    cos_ref[...] = jnp.cos(ang)
    sin_ref[...] = jnp.sin(ang)


def _rope_tables(pos_t, inv_t):
    half, t = pos_t.shape
    tr = inv_t.shape[1]
    col = pl.BlockSpec((half, tr), lambda i: (0, i))
    return pl.pallas_call(
        _rope_table_kernel,
        grid=(t // tr,),
        in_specs=[col, pl.BlockSpec((half, tr), lambda i: (0, 0))],
        out_specs=[col, col],
        out_shape=[jax.ShapeDtypeStruct((half, t), F32)] * 2,
        compiler_params=_params(("parallel",)),
        name="rope_tables",
    )(pos_t, inv_t)


def _adaln_kernel(c_ref, w_ref, b_ref, o_ref):
    c_act = _silu(c_ref[...]).astype(BF16)
    acc = jnp.dot(c_act, w_ref[0].astype(BF16), preferred_element_type=F32)
    o_ref[0] = acc + b_ref[0]


def _adaln(c_pad, ada_w, ada_b3, tn):
    depth, d, n = ada_w.shape
    rows = c_pad.shape[0]
    return pl.pallas_call(
        _adaln_kernel,
        grid=(depth, n // tn),
        in_specs=[
            pl.BlockSpec((rows, d), lambda l, j: (0, 0)),
            pl.BlockSpec((1, d, tn), lambda l, j: (l, 0, j)),
            pl.BlockSpec((1, 1, tn), lambda l, j: (l, 0, j)),
        ],
        out_specs=pl.BlockSpec((1, rows, tn), lambda l, j: (l, 0, j)),
        out_shape=jax.ShapeDtypeStruct((depth, rows, n), F32),
        compiler_params=_params(("parallel", "parallel")),
        name="adaln_mod",
    )(c_pad, ada_w, ada_b3)


def _wprep_kernel(wt_ref, o_ref, *, segments):
    row = 0
    for a, b in segments:
        o_ref[0, row:row + (b - a), :] = wt_ref[0, a:b, :].astype(BF16)
        row += b - a
    o_ref[0, row:, :] = jnp.zeros((o_ref.shape[1] - row, o_ref.shape[2]), BF16)


def _wprep(w_in_t, segments, cb):
    depth, n, d = w_in_t.shape
    return pl.pallas_call(
        functools.partial(_wprep_kernel, segments=segments),
        grid=(depth, d // cb),
        in_specs=[pl.BlockSpec((1, n, cb), lambda l, i: (l, 0, i))],
        out_specs=pl.BlockSpec((1, Z_COLS, cb), lambda l, i: (l, 0, i)),
        out_shape=jax.ShapeDtypeStruct((depth, Z_COLS, d), BF16),
        compiler_params=_params(("parallel", "parallel")),
        name="w_in_layout",
    )(w_in_t)


def _inproj_kernel(x_ref, shift_ref, scale_ref, g_ref, w_ref, z_ref, h0_ref, h1_ref, *, nm, nn):
    i, j = pl.program_id(0), pl.program_id(1)
    tm, d = x_ref.shape
    rows_per_step = tm // nn
    h_refs = (h0_ref, h1_ref)

    def normalise(h_ref):
        rows = pl.ds(pl.multiple_of(j * rows_per_step, rows_per_step), rows_per_step)
        x = x_ref[rows, :]
        inv = lax.rsqrt(jnp.sum(x * x, axis=-1, keepdims=True) * (1.0 / d) + EPS)
        h_ref[rows, :] = (x * inv * (g_ref[...] * (1.0 + scale_ref[0])) + shift_ref[0]).astype(BF16)

    def project(h_ref):
        z_ref[...] = lax.dot_general(h_ref[...], w_ref[0], _NT, preferred_element_type=F32).astype(BF16)

    @pl.when(i == 0)
    def _():
        normalise(h_refs[0])

    for parity in range(2):
        @pl.when(jnp.logical_and(jnp.logical_and(i > 0, i < nm), i % 2 == parity))
        def _():
            project(h_refs[1 - parity])
            normalise(h_refs[parity])

    @pl.when(i == nm)
    def _():
        project(h_refs[(nm - 1) % 2])


def _inproj(x2, mod3, g, w_t, layer, seq, tm, tn):
    t, d = x2.shape
    ncol = w_t.shape[1]
    nm, nn = t // tm, ncol // tn
    tiles_per_seq = seq // tm
    assert tm % nn == 0

    def x_tile(i):
        return jnp.minimum(i, nm - 1)

    def mod_spec(chunk):
        return pl.BlockSpec((1, 1, d), lambda i, j: (x_tile(i) // tiles_per_seq, 0, chunk))

    def w_tile(i, j):
        return jnp.where(i == 0, 0, j)

    return pl.pallas_call(
        functools.partial(_inproj_kernel, nm=nm, nn=nn),
        grid=(nm + 1, nn),
        in_specs=[
            pl.BlockSpec((tm, d), lambda i, j: (x_tile(i), 0)),
            mod_spec(0),
            mod_spec(1),
            pl.BlockSpec((1, d), lambda i, j: (0, 0)),
            pl.BlockSpec((1, tn, d), lambda i, j: (layer, w_tile(i, j), 0)),
        ],
        out_specs=pl.BlockSpec((tm, tn), lambda i, j: (jnp.maximum(i - 1, 0), w_tile(i, j))),
        out_shape=jax.ShapeDtypeStruct((t, ncol), BF16),
        scratch_shapes=[pltpu.VMEM((tm, d), BF16), pltpu.VMEM((tm, d), BF16)],
        compiler_params=_params(("arbitrary", "arbitrary")),
        name="inproj",
    )(x2, mod3, mod3, g, w_t)


def _rms_rows(x, g):
    inv = lax.rsqrt(jnp.mean(x * x, axis=-1, keepdims=True) + EPS)
    return x * inv * g


def _rotate_half_pairs(x):
    lane = lax.broadcasted_iota(jnp.int32, x.shape, 1)
    first = (lane % ROPE_DIM) < (ROPE_DIM // 2)
    return jnp.where(first, pltpu.roll(x, LANES - ROPE_DIM // 2, 1), pltpu.roll(x, ROPE_DIM // 2, 1))


def _qkv_kernel(qlat_ref, kvlat_ref, krope_ref, cost_ref, sint_ref, gql_ref, gkvl_ref,
                wqt_ref, wk_ref, wvt_ref, gq_ref, gkn_ref, gkr_ref, qt_ref, k_ref, vt_ref):
    qn = _rms_rows(qlat_ref[...].astype(F32), gql_ref[...]).astype(BF16)
    kvn = _rms_rows(kvlat_ref[...].astype(F32), gkvl_ref[...]).astype(BF16)
    qt = lax.dot_general(wqt_ref[...], qn, _NT, preferred_element_type=F32)
    kn = jnp.dot(kvn, wk_ref[...], preferred_element_type=F32)
    vt = lax.dot_general(wvt_ref[...], kvn, _NT, preferred_element_type=F32)

    cost = cost_ref[...]
    sint = sint_ref[...]
    gq = gq_ref[...]
    half = ROPE_DIM // 2
    q_scale = math.log2(math.e) / math.sqrt(QK_DIM)
    for h in range(N_HEADS):
        blk = qt[h * QK_DIM:(h + 1) * QK_DIM, :]
        inv = lax.rsqrt(jnp.sum(blk * blk, axis=0, keepdims=True) * (1.0 / QK_DIM) + EPS) * q_scale
        y = blk * gq
        x1 = y[NOPE_DIM:NOPE_DIM + half, :]
        x2 = y[NOPE_DIM + half:QK_DIM, :]
        qt_ref[0, h, 0:NOPE_DIM, :] = (y[0:NOPE_DIM, :] * inv).astype(BF16)
        qt_ref[0, h, NOPE_DIM:NOPE_DIM + half, :] = ((x1 * cost - x2 * sint) * inv).astype(BF16)
        qt_ref[0, h, NOPE_DIM + half:QK_DIM, :] = ((x2 * cost + x1 * sint) * inv).astype(BF16)

    kr = krope_ref[...].astype(F32)
    kr_ss = jnp.sum(kr * kr, axis=-1, keepdims=True)
    krg = kr * gkr_ref[...]
    cos_tok = jnp.concatenate([cost] * (LANES // half), axis=0).T
    sin_tok = jnp.concatenate([-sint, sint] * (LANES // ROPE_DIM), axis=0).T
    kr_rot = krg * cos_tok + _rotate_half_pairs(krg) * sin_tok
    ones_rows = jnp.ones((VT_ROWS - V_DIM, vt.shape[1]), BF16)
    for h in range(N_HEADS):
        k_nope = kn[:, h * NOPE_DIM:(h + 1) * NOPE_DIM]
        ss = jnp.sum(k_nope * k_nope, axis=-1, keepdims=True) + kr_ss
        inv = lax.rsqrt(ss * (1.0 / QK_DIM) + EPS)
        k_ref[0, h, :, 0:NOPE_DIM] = (k_nope * inv * gkn_ref[...]).astype(BF16)
        k_ref[0, h, :, NOPE_DIM:QK_DIM] = (kr_rot[:, 0:ROPE_DIM] * inv).astype(BF16)
        vt_ref[0, h, 0:V_DIM, :] = vt[h * V_DIM:(h + 1) * V_DIM, :].astype(BF16)
        vt_ref[0, h, V_DIM:VT_ROWS, :] = ones_rows


def _qkv_up(z, cos_ft, sin_ft, gql, gkvl, wqt, wk, wvt, gq, gkn, gkr, batch, seq, tm):
    tiles_per_seq = seq // tm

    def zcol(width, offset):
        return pl.BlockSpec((tm, width), lambda i: (i, offset // width))

    def full(arr):
        return pl.BlockSpec(arr.shape, lambda i: (0,) * arr.ndim)

    def feat_major(rows):
        return pl.BlockSpec((1, N_HEADS, rows, tm), lambda i: (i // tiles_per_seq, 0, 0, i % tiles_per_seq))

    k_out = pl.BlockSpec((1, N_HEADS, tm, QK_DIM), lambda i: (i // tiles_per_seq, 0, i % tiles_per_seq, 0))
    col = pl.BlockSpec((ROPE_DIM // 2, tm), lambda i: (0, i))
    return pl.pallas_call(
        _qkv_kernel,
        grid=(batch * tiles_per_seq,),
        in_specs=[zcol(Q_LORA, Z_Q_LAT), zcol(KV_LORA, Z_KV_LAT), zcol(LANES, Z_K_ROPE), col, col,
                  full(gql), full(gkvl), full(wqt), full(wk), full(wvt), full(gq), full(gkn), full(gkr)],
        out_specs=[feat_major(QK_DIM), k_out, feat_major(VT_ROWS)],
        out_shape=[jax.ShapeDtypeStruct((batch, N_HEADS, QK_DIM, seq), BF16),
                   jax.ShapeDtypeStruct((batch, N_HEADS, seq, QK_DIM), BF16),
                   jax.ShapeDtypeStruct((batch, N_HEADS, VT_ROWS, seq), BF16)],
        compiler_params=_params(("parallel",)),
        name="qkv_up",
    )(z, z, z, cos_ft, sin_ft, gql, gkvl, wqt, wk, wvt, gq, gkn, gkr)


def _attn_kernel(qt_ref, k_ref, vt_ref, gate_ref, o_ref, m_ref, alpha_ref, acc_ref, p_ref, *, tq, qc, hp):
    tk = tq
    qi = pl.program_id(2)
    m_ref[...] = jnp.full(m_ref.shape, -jnp.inf, F32)
    acc_ref[...] = jnp.zeros(acc_ref.shape, F32)
    chains = tuple((h, c) for h in range(hp) for c in range(tq // qc))

    def qcols(c):
        return slice(c * qc, (c + 1) * qc)

    def keys_of(kb):
        return pl.ds(pl.multiple_of(kb * tk, tk), tk)

    def qk(k, h, c):
        return jnp.dot(k, qt_ref[0, h, :, qcols(c)], preferred_element_type=F32)

    def softmax(h, c, parts):
        m_prev = m_ref[h, :, qcols(c)]
        m_new = m_prev
        for _, st in parts:
            m_new = jnp.maximum(m_new, jnp.max(st, axis=0, keepdims=True))
        alpha_ref[h, :, qcols(c)] = jnp.exp2(m_prev - m_new)
        for r0, st in parts:
            p_ref[h, r0:r0 + st.shape[0], qcols(c)] = jnp.exp2(st - m_new).astype(BF16)
        m_ref[h, :, qcols(c)] = m_new

    def pv(kb):
        vts = [vt_ref[0, h, :, keys_of(kb)] for h in range(hp)]
        for h, c in chains:
            acc_ref[h, :, qcols(c)] = (alpha_ref[h, :, qcols(c)] * acc_ref[h, :, qcols(c)]
                                       + jnp.dot(vts[h], p_ref[h, :, qcols(c)], preferred_element_type=F32))

    n_full = qi
    d0 = pl.multiple_of(qi * tq, tq)
    sts = [qk(k_ref[0, h, pl.ds(d0, (c + 1) * qc), :], h, c) for h, c in chains]
    tri = (lax.broadcasted_iota(jnp.int32, (qc, qc), 0) <= lax.broadcasted_iota(jnp.int32, (qc, qc), 1))
    for (h, c), st in zip(chains, sts):
        parts = [(c * qc, jnp.where(tri, st[c * qc:, :], -jnp.inf))]
        if c:
            parts.insert(0, (0, st[:c * qc, :]))
        softmax(h, c, parts)
        if (c + 1) * qc < tk:
            p_ref[h, (c + 1) * qc:tk, qcols(c)] = jnp.zeros((tk - (c + 1) * qc, qc), BF16)

    def body(kb, carry):
        ks = [k_ref[0, h, keys_of(kb), :] for h in range(hp)]
        sts = [qk(ks[h], h, c) for h, c in chains]
        pv(jnp.where(kb == 0, n_full, kb - 1))
        for (h, c), st in zip(chains, sts):
            softmax(h, c, [(0, st)])
        return carry

    lax.fori_loop(0, n_full, body, 0)
    pv(jnp.where(n_full == 0, n_full, n_full - 1))

    for h in range(hp):
        acc = acc_ref[h]
        out_t = acc[0:V_DIM, :] / acc[V_DIM:V_DIM + 1, :]
        gate = gate_ref[:, h * V_DIM:(h + 1) * V_DIM].astype(F32)
        o_ref[:, h * V_DIM:(h + 1) * V_DIM] = (out_t.T * _silu(gate)).astype(BF16)


def _attention(qt, k, vt, z, tq, hp):
    batch, heads, _, seq = qt.shape
    nq = seq // tq
    gate_col0 = Z_MLA_GATE // (hp * V_DIM)
    return pl.pallas_call(
        functools.partial(_attn_kernel, tq=tq, qc=min(tq, 256), hp=hp),
        grid=(batch, heads // hp, nq),
        in_specs=[
            pl.BlockSpec((1, hp, QK_DIM, tq), lambda b, h, i: (b, h, 0, i)),
            pl.BlockSpec((1, hp, seq, QK_DIM), lambda b, h, i: (b, h, 0, 0)),
            pl.BlockSpec((1, hp, VT_ROWS, seq), lambda b, h, i: (b, h, 0, 0)),
            pl.BlockSpec((tq, hp * V_DIM), lambda b, h, i: (b * nq + i, gate_col0 + h)),
        ],
        out_specs=pl.BlockSpec((tq, hp * V_DIM), lambda b, h, i: (b * nq + i, h)),
        out_shape=jax.ShapeDtypeStruct((batch * seq, heads * V_DIM), BF16),
        scratch_shapes=[pltpu.VMEM((hp, 1, tq), F32), pltpu.VMEM((hp, 1, tq), F32), pltpu.VMEM((hp, VT_ROWS, tq), F32),
                        pltpu.VMEM((hp, tq, tq), BF16)],
        compiler_params=_params(("parallel", "parallel", "arbitrary")),
        name="mla_attention",
    )(qt, k, vt, z)


HALO = 32
SUBLANES = 8
PITCH = 100
PHASES = 8


def _conv_kernel(uval_ref, ugate_ref, cgate_ref, glub_ref, dww_ref, dwb_ref, lng_ref, lnb_ref, wpw_ref, bpw_ref,
                 o_ref, ubuf_ref, cbuf_ref, carry_ref):
    ts, c = uval_ref.shape
    seg = ts // SUBLANES
    n_slabs = c // LANES
    first = HALO - (CONV_K - 1)

    @pl.when(pl.program_id(1) == 0)
    def _():
        carry_ref[...] = jnp.zeros(carry_ref.shape, F32)

    a = uval_ref[...].astype(F32) + glub_ref[:, 0:c]
    g = ugate_ref[...].astype(F32) + glub_ref[:, c:2 * c]
    u = a * _sigmoid(g)
    for s in range(n_slabs):
        us = u[:, s * LANES:(s + 1) * LANES]
        ubuf_ref[s, first:HALO, :] = carry_ref[s, first:HALO, :]
        for k in range(SUBLANES):
            ubuf_ref[s, k * PITCH + HALO:k * PITCH + HALO + seg, :] = us[k * seg:(k + 1) * seg]
            if k + 1 < SUBLANES:
                ubuf_ref[s, (k + 1) * PITCH + first:(k + 1) * PITCH + HALO, :] = (
                    us[(k + 1) * seg - (HALO - first):(k + 1) * seg])
        carry_ref[s, first:HALO, :] = us[ts - (HALO - first):ts]

    def slab_body(s, carry):
        bias = dwb_ref[s]

        def group_body(gi, carry2):
            b0 = gi * PHASES
            accs = [None] * PHASES
            for r in range(CONV_K - 1 + PHASES):
                rows = ubuf_ref[s, pl.ds(b0 + first + r, SUBLANES, stride=PITCH), :]
                for ph in range(PHASES):
                    j = r - ph
                    if 0 <= j < CONV_K:
                        term = dww_ref[s, j:j + 1, :] * rows
                        accs[ph] = term if accs[ph] is None else accs[ph] + term
            for ph in range(PHASES):
                cbuf_ref[s, pl.ds(b0 + ph, SUBLANES, stride=seg), :] = accs[ph] + bias
            return carry2

        return lax.fori_loop(0, seg // PHASES, group_body, carry, unroll=True)

    lax.fori_loop(0, n_slabs, slab_body, 0)

    y = jnp.concatenate([cbuf_ref[s] for s in range(n_slabs)], axis=-1)
    mu = jnp.mean(y, axis=-1, keepdims=True)
    yc = y - mu
    var = jnp.mean(yc * yc, axis=-1, keepdims=True)
    yn = yc * lax.rsqrt(var + EPS) * lng_ref[...] + lnb_ref[...]
    act = _silu(yn).astype(BF16)
    pw = jnp.dot(act, wpw_ref[...], preferred_element_type=F32) + bpw_ref[...]
    o_ref[...] = (pw * _silu(cgate_ref[...].astype(F32))).astype(BF16)


def _conv_module(z, glub, dww, dwb, lng, lnb, wpw, bpw, batch, seq, ts):
    c = wpw.shape[0]
    n_slabs = c // LANES
    tiles_per_seq = seq // ts
    assert ts % (SUBLANES * PHASES) == 0 and HALO + ts // SUBLANES <= PITCH

    def zcol(offset):
        return pl.BlockSpec((ts, c), lambda b, i: (b * tiles_per_seq + i, offset // c))

    def full(arr):
        return pl.BlockSpec(arr.shape, lambda b, i: (0,) * arr.ndim)

    return pl.pallas_call(
        _conv_kernel,
        grid=(batch, tiles_per_seq),
        in_specs=[zcol(Z_CONV_IN), zcol(Z_CONV_IN + c), zcol(Z_CONV_GATE),
                  full(glub), full(dww), full(dwb), full(lng), full(lnb), full(wpw), full(bpw)],
        out_specs=pl.BlockSpec((ts, c), lambda b, i: (b * tiles_per_seq + i, 0)),
        out_shape=jax.ShapeDtypeStruct((batch * seq, c), BF16),
        scratch_shapes=[pltpu.VMEM((n_slabs, SUBLANES * PITCH, LANES), F32), pltpu.VMEM((n_slabs, ts, LANES), F32),
                        pltpu.VMEM((n_slabs, HALO, LANES), F32)],
        compiler_params=_params(("parallel", "arbitrary")),
        name="conv_module",
    )(z, z, z, glub, dww, dwb, lng, lnb, wpw, bpw)


def _slabs(w):
    rows, c = w.shape
    return w.reshape(rows, c // LANES, LANES).transpose(1, 0, 2)


def _outproj_kernel(mla_ref, conv_ref, x_ref, gate_ref, wa_ref, wb_ref, o_ref):
    y = jnp.dot(mla_ref[...], wa_ref[...], preferred_element_type=F32)
    y = y + jnp.dot(conv_ref[...], wb_ref[...], preferred_element_type=F32)
    o_ref[...] = x_ref[...] + gate_ref[0] * y


def _outproj(mla, conv, x2, mod3, w_a, w_b, seq, tm):
    t, d = x2.shape
    half = mla.shape[1]
    tiles_per_seq = seq // tm
    return pl.pallas_call(
        _outproj_kernel,
        grid=(t // tm,),
        in_specs=[
            pl.BlockSpec((tm, half), lambda i: (i, 0)),
            pl.BlockSpec((tm, half), lambda i: (i, 0)),
            pl.BlockSpec((tm, d), lambda i: (i, 0)),
            pl.BlockSpec((1, 1, d), lambda i: (i // tiles_per_seq, 0, 2)),
            pl.BlockSpec((half, d), lambda i: (0, 0)),
            pl.BlockSpec((half, d), lambda i: (0, 0)),
        ],
        out_specs=pl.BlockSpec((tm, d), lambda i: (i, 0)),
        out_shape=jax.ShapeDtypeStruct((t, d), F32),
        compiler_params=_params(("parallel",)),
        name="outproj",
    )(mla, conv, x2, mod3, w_a, w_b)


def _tile(n, want):
    t = min(n, want)
    assert n % t == 0, (n, t)
    return t


def kernel(x, c, positions, ada_w, ada_b, norm_g, w_in, q_lat_g, w_q_up, kv_lat_g, w_kv_up, q_norm_g, k_norm_g,
           glu_b, dw_w, dw_b, conv_ln_g, conv_ln_b, w_pw, b_pw, w_out):
    batch, seq, d = x.shape
    depth = ada_w.shape[0]
    t = batch * seq
    d_mla = N_HEADS * V_DIM
    d_conv = w_pw.shape[1]
    assert w_in.shape[2] == Q_LORA + KV_LORA + ROPE_DIM + d_mla + 3 * d_conv
    assert d_mla == 1024 and d_conv == 1024 and d == 2048

    inv_freq = 1.0 / (ROPE_THETA ** (jnp.arange(0, ROPE_DIM, 2, dtype=F32) / ROPE_DIM))
    pos_t = jnp.broadcast_to(positions.reshape(1, t), (ROPE_DIM // 2, t))
    inv_t = jnp.broadcast_to(inv_freq[:, None], (ROPE_DIM // 2, _tile(t, 2048)))
    cos_ft, sin_ft = _rope_tables(pos_t, inv_t)

    c_rows = 8
    c_pad = jnp.zeros((c_rows, d), F32).at[:batch].set(c)
    mod = _adaln(c_pad, ada_w, ada_b.reshape(depth, 1, 3 * d), _tile(3 * d, 1024))

    x2 = x.reshape(t, d)
    s0 = Q_LORA
    s1 = s0 + KV_LORA
    s2 = s1 + ROPE_DIM
    s3 = s2 + d_mla
    s4 = s3 + 2 * d_conv
    w_z_t = _wprep(jnp.swapaxes(w_in, 1, 2), ((s3, s4), (s2, s3), (s4, w_in.shape[2]), (0, s0), (s0, s1), (s1, s2)), 256)
    for l in range(depth):
        mod3 = mod[l].reshape(c_rows, 1, 3 * d)
        wqt = w_q_up[l].T.astype(BF16)
        wkv = w_kv_up[l].reshape(KV_LORA, N_HEADS, NOPE_DIM + V_DIM)
        wk = wkv[:, :, :NOPE_DIM].reshape(KV_LORA, N_HEADS * NOPE_DIM).astype(BF16)
        wvt = wkv[:, :, NOPE_DIM:].reshape(KV_LORA, N_HEADS * V_DIM).T.astype(BF16)
        gq, gk = q_norm_g[l], k_norm_g[l]
        gkn = gk[None, :NOPE_DIM]
        gkr = jnp.concatenate([gk[NOPE_DIM:], jnp.zeros((LANES - ROPE_DIM,), F32)])[None, :]

        z = _inproj(x2, mod3, norm_g[l][None, :], w_z_t, l, seq, _tile(seq, 1024), 1280)
        tm_qkv = _tile(seq, 512)
        qt, k, vt = _qkv_up(z, cos_ft, sin_ft, q_lat_g[l][None, :], kv_lat_g[l][None, :], wqt, wk, wvt,
                            jnp.broadcast_to(gq[:, None], (QK_DIM, tm_qkv)), gkn, gkr, batch, seq, tm_qkv)
        mla = _attention(qt, k, vt, z, _tile(seq, 1024), 4)
        dww = jnp.concatenate([dw_w[l], jnp.zeros((HALO - CONV_K, d_conv), F32)], axis=0)
        conv = _conv_module(z, glu_b[l][None, :], _slabs(dww), _slabs(dw_b[l][None, :]), conv_ln_g[l][None, :],
                            conv_ln_b[l][None, :], w_pw[l].astype(BF16), b_pw[l][None, :], batch, seq,
                            _tile(seq, 512))
        wo = w_out[l].astype(BF16)
        x2 = _outproj(mla, conv, x2, mod3, wo[:d_mla], wo[d_mla:], seq, _tile(seq, 512))
    return x2.reshape(batch, seq, d)
```

```python
import functools
import math

import jax
import jax.numpy as jnp
from jax import lax
from jax.experimental import pallas as pl
from jax.experimental.pallas import tpu as pltpu

F32 = jnp.float32
BF16 = jnp.bfloat16

N_HEADS = 8
NOPE_DIM = 128
ROPE_DIM = 64
V_DIM = 128
QK_DIM = NOPE_DIM + ROPE_DIM
Q_LORA = 512
KV_LORA = 256
ROPE_THETA = 10000.0
CONV_K = 31
EPS = 1e-6
LANES = 128
BF16_SUBLANES = 16
VT_ROWS = V_DIM + BF16_SUBLANES
_NT = (((1,), (1,)), ((), ()))

Z_CONV_IN = 0
Z_MLA_GATE = 2048
Z_CONV_GATE = 3072
Z_Q_LAT = 4096
Z_KV_LAT = 4608
Z_K_ROPE = 4864
Z_COLS = 5120

VMEM_LIMIT = 56 * 1024 * 1024


def _sigmoid(x):
    return 1.0 / (1.0 + jnp.exp(-x))


def _silu(x):
    return x * _sigmoid(x)


def _params(sem, vmem=VMEM_LIMIT):
    return pltpu.CompilerParams(dimension_semantics=sem, vmem_limit_bytes=vmem)


def _rope_table_kernel(pos_ref, inv_ref, cos_ref, sin_ref):
    ang = pos_ref[...].astype(F32) * inv_ref[...]
    cos_ref[...] = jnp.cos(ang)
    sin_ref[...] = jnp.sin(ang)


def _rope_tables(pos_t, inv_t):
    half, t = pos_t.shape
    tr = inv_t.shape[1]
    col = pl.BlockSpec((half, tr), lambda i: (0, i))
    return pl.pallas_call(
        _rope_table_kernel,
        grid=(t // tr,),
        in_specs=[col, pl.BlockSpec((half, tr), lambda i: (0, 0))],
        out_specs=[col, col],
        out_shape=[jax.ShapeDtypeStruct((half, t), F32)] * 2,
        compiler_params=_params(("parallel",)),
        name="rope_tables",
    )(pos_t, inv_t)


def _adaln_kernel(c_ref, w_ref, b_ref, o_ref):
    c_act = _silu(c_ref[...]).astype(BF16)
    acc = jnp.dot(c_act, w_ref[0].astype(BF16), preferred_element_type=F32)
    o_ref[0] = acc + b_ref[0]


def _adaln(c_pad, ada_w, ada_b3, tn):
    depth, d, n = ada_w.shape
    rows = c_pad.shape[0]
    return pl.pallas_call(
        _adaln_kernel,
        grid=(depth, n // tn),
        in_specs=[
            pl.BlockSpec((rows, d), lambda l, j: (0, 0)),
            pl.BlockSpec((1, d, tn), lambda l, j: (l, 0, j)),
            pl.BlockSpec((1, 1, tn), lambda l, j: (l, 0, j)),
        ],
        out_specs=pl.BlockSpec((1, rows, tn), lambda l, j: (l, 0, j)),
        out_shape=jax.ShapeDtypeStruct((depth, rows, n), F32),
        compiler_params=_params(("parallel", "parallel")),
        name="adaln_mod",
    )(c_pad, ada_w, ada_b3)


def _wprep_kernel(wt_ref, o_ref, *, segments):
    row = 0
    for a, b in segments:
        o_ref[0, row:row + (b - a), :] = wt_ref[0, a:b, :].astype(BF16)
        row += b - a
    o_ref[0, row:, :] = jnp.zeros((o_ref.shape[1] - row, o_ref.shape[2]), BF16)


def _wprep(w_in_t, segments, cb):
    depth, n, d = w_in_t.shape
    return pl.pallas_call(
        functools.partial(_wprep_kernel, segments=segments),
        grid=(depth, d // cb),
        in_specs=[pl.BlockSpec((1, n, cb), lambda l, i: (l, 0, i))],
        out_specs=pl.BlockSpec((1, Z_COLS, cb), lambda l, i: (l, 0, i)),
        out_shape=jax.ShapeDtypeStruct((depth, Z_COLS, d), BF16),
        compiler_params=_params(("parallel", "parallel")),
        name="w_in_layout",
    )(w_in_t)


def _inproj_kernel(x_ref, shift_ref, scale_ref, g_ref, w_ref, z_ref, h0_ref, h1_ref, *, nm, nn):
    i, j = pl.program_id(0), pl.program_id(1)
    tm, d = x_ref.shape
    rows_per_step = tm // nn
    h_refs = (h0_ref, h1_ref)

    def normalise(h_ref):
        rows = pl.ds(pl.multiple_of(j * rows_per_step, rows_per_step), rows_per_step)
        x = x_ref[rows, :]
        inv = lax.rsqrt(jnp.sum(x * x, axis=-1, keepdims=True) * (1.0 / d) + EPS)
        h_ref[rows, :] = (x * inv * (g_ref[...] * (1.0 + scale_ref[0])) + shift_ref[0]).astype(BF16)

    def project(h_ref):
        z_ref[...] = lax.dot_general(h_ref[...], w_ref[0], _NT, preferred_element_type=F32).astype(BF16)

    @pl.when(i == 0)
    def _():
        normalise(h_refs[0])

    for parity in range(2):
        @pl.when(jnp.logical_and(jnp.logical_and(i > 0, i < nm), i % 2 == parity))
        def _():
            project(h_refs[1 - parity])
            normalise(h_refs[parity])

    @pl.when(i == nm)
    def _():
        project(h_refs[(nm - 1) % 2])


def _inproj(x2, mod3, g, w_t, layer, seq, tm, tn):
    t, d = x2.shape
    ncol = w_t.shape[1]
    nm, nn = t // tm, ncol // tn
    tiles_per_seq = seq // tm
    assert tm % nn == 0

    def x_tile(i):
        return jnp.minimum(i, nm - 1)

    def mod_spec(chunk):
        return pl.BlockSpec((1, 1, d), lambda i, j: (x_tile(i) // tiles_per_seq, 0, chunk))

    def w_tile(i, j):
        return jnp.where(i == 0, 0, j)

    return pl.pallas_call(
        functools.partial(_inproj_kernel, nm=nm, nn=nn),
        grid=(nm + 1, nn),
        in_specs=[
            pl.BlockSpec((tm, d), lambda i, j: (x_tile(i), 0)),
            mod_spec(0),
            mod_spec(1),
            pl.BlockSpec((1, d), lambda i, j: (0, 0)),
            pl.BlockSpec((1, tn, d), lambda i, j: (layer, w_tile(i, j), 0),
                         pipeline_mode=pl.Buffered(1) if nn == 1 else None),
        ],
        out_specs=pl.BlockSpec((tm, tn), lambda i, j: (jnp.maximum(i - 1, 0), w_tile(i, j))),
        out_shape=jax.ShapeDtypeStruct((t, ncol), BF16),
        scratch_shapes=[pltpu.VMEM((tm, d), BF16), pltpu.VMEM((tm, d), BF16)],
        compiler_params=_params(("arbitrary", "arbitrary")),
        name="inproj",
    )(x2, mod3, mod3, g, w_t)


def _rms_rows(x, g):
    inv = lax.rsqrt(jnp.mean(x * x, axis=-1, keepdims=True) + EPS)
    return x * inv * g


def _rotate_half_pairs(x):
    lane = lax.broadcasted_iota(jnp.int32, x.shape, 1)
    first = (lane % ROPE_DIM) < (ROPE_DIM // 2)
    return jnp.where(first, pltpu.roll(x, LANES - ROPE_DIM // 2, 1), pltpu.roll(x, ROPE_DIM // 2, 1))


def _qkv_kernel(qlat_ref, kvlat_ref, krope_ref, cost_ref, sint_ref, gql_ref, gkvl_ref,
                wqt_ref, wk_ref, wvt_ref, gq_ref, gkn_ref, gkr_ref, qt_ref, k_ref, vt_ref):
    qn = _rms_rows(qlat_ref[...].astype(F32), gql_ref[...]).astype(BF16)
    kvn = _rms_rows(kvlat_ref[...].astype(F32), gkvl_ref[...]).astype(BF16)
    qt = lax.dot_general(wqt_ref[...], qn, _NT, preferred_element_type=F32)
    kn = jnp.dot(kvn, wk_ref[...], preferred_element_type=F32)
    vt = lax.dot_general(wvt_ref[...], kvn, _NT, preferred_element_type=F32)

    cost = cost_ref[...]
    sint = sint_ref[...]
    gq = gq_ref[...]
    half = ROPE_DIM // 2
    q_scale = math.log2(math.e) / math.sqrt(QK_DIM)
    for h in range(N_HEADS):
        blk = qt[h * QK_DIM:(h + 1) * QK_DIM, :]
        inv = lax.rsqrt(jnp.sum(blk * blk, axis=0, keepdims=True) * (1.0 / QK_DIM) + EPS) * q_scale
        y = blk * gq
        x1 = y[NOPE_DIM:NOPE_DIM + half, :]
        x2 = y[NOPE_DIM + half:QK_DIM, :]
        qt_ref[0, h, 0:NOPE_DIM, :] = (y[0:NOPE_DIM, :] * inv).astype(BF16)
        qt_ref[0, h, NOPE_DIM:NOPE_DIM + half, :] = ((x1 * cost - x2 * sint) * inv).astype(BF16)
        qt_ref[0, h, NOPE_DIM + half:QK_DIM, :] = ((x2 * cost + x1 * sint) * inv).astype(BF16)

    kr = krope_ref[...].astype(F32)
    kr_ss = jnp.sum(kr * kr, axis=-1, keepdims=True)
    krg = kr * gkr_ref[...]
    cos_tok = jnp.concatenate([cost] * (LANES // half), axis=0).T
    sin_tok = jnp.concatenate([-sint, sint] * (LANES // ROPE_DIM), axis=0).T
    kr_rot = krg * cos_tok + _rotate_half_pairs(krg) * sin_tok
    ones_rows = jnp.ones((VT_ROWS - V_DIM, vt.shape[1]), BF16)
    for h in range(N_HEADS):
        k_nope = kn[:, h * NOPE_DIM:(h + 1) * NOPE_DIM]
        ss = jnp.sum(k_nope * k_nope, axis=-1, keepdims=True) + kr_ss
        inv = lax.rsqrt(ss * (1.0 / QK_DIM) + EPS)
        k_ref[0, h, :, 0:NOPE_DIM] = (k_nope * inv * gkn_ref[...]).astype(BF16)
        k_ref[0, h, :, NOPE_DIM:QK_DIM] = (kr_rot[:, 0:ROPE_DIM] * inv).astype(BF16)
        vt_ref[0, h, 0:V_DIM, :] = vt[h * V_DIM:(h + 1) * V_DIM, :].astype(BF16)
        vt_ref[0, h, V_DIM:VT_ROWS, :] = ones_rows


def _qkv_up(z, cos_ft, sin_ft, gql, gkvl, wqt, wk, wvt, gq, gkn, gkr, batch, seq, tm):
    tiles_per_seq = seq // tm

    def zcol(width, offset):
        return pl.BlockSpec((tm, width), lambda i: (i, offset // width))

    def full(arr):
        return pl.BlockSpec(arr.shape, lambda i: (0,) * arr.ndim)

    def feat_major(rows):
        return pl.BlockSpec((1, N_HEADS, rows, tm), lambda i: (i // tiles_per_seq, 0, 0, i % tiles_per_seq))

    k_out = pl.BlockSpec((1, N_HEADS, tm, QK_DIM), lambda i: (i // tiles_per_seq, 0, i % tiles_per_seq, 0))
    col = pl.BlockSpec((ROPE_DIM // 2, tm), lambda i: (0, i))
    return pl.pallas_call(
        _qkv_kernel,
        grid=(batch * tiles_per_seq,),
        in_specs=[zcol(Q_LORA, Z_Q_LAT), zcol(KV_LORA, Z_KV_LAT), zcol(LANES, Z_K_ROPE), col, col,
                  full(gql), full(gkvl), full(wqt), full(wk), full(wvt), full(gq), full(gkn), full(gkr)],
        out_specs=[feat_major(QK_DIM), k_out, feat_major(VT_ROWS)],
        out_shape=[jax.ShapeDtypeStruct((batch, N_HEADS, QK_DIM, seq), BF16),
                   jax.ShapeDtypeStruct((batch, N_HEADS, seq, QK_DIM), BF16),
                   jax.ShapeDtypeStruct((batch, N_HEADS, VT_ROWS, seq), BF16)],
        compiler_params=_params(("parallel",)),
        name="qkv_up",
    )(z, z, z, cos_ft, sin_ft, gql, gkvl, wqt, wk, wvt, gq, gkn, gkr)


def _attn_kernel(qt_ref, k_ref, vt_ref, gate_ref, o_ref, m_ref, alpha_ref, acc_ref, p_ref, *, tq, qc, hp):
    tk = tq
    qi = pl.program_id(2)
    m_ref[...] = jnp.full(m_ref.shape, -jnp.inf, F32)
    acc_ref[...] = jnp.zeros(acc_ref.shape, F32)
    chains = tuple((h, c) for h in range(hp) for c in range(tq // qc))

    def qcols(c):
        return slice(c * qc, (c + 1) * qc)

    def keys_of(kb):
        return pl.ds(pl.multiple_of(kb * tk, tk), tk)

    def qk(k, h, c):
        return jnp.dot(k, qt_ref[0, h, :, qcols(c)], preferred_element_type=F32)

    def softmax(h, c, parts):
        m_prev = m_ref[h, :, qcols(c)]
        m_new = m_prev
        for _, st in parts:
            m_new = jnp.maximum(m_new, jnp.max(st, axis=0, keepdims=True))
        alpha_ref[h, :, qcols(c)] = jnp.exp2(m_prev - m_new)
        for r0, st in parts:
            p_ref[h, r0:r0 + st.shape[0], qcols(c)] = jnp.exp2(st - m_new).astype(BF16)
        m_ref[h, :, qcols(c)] = m_new

    def pv(kb):
        vts = [vt_ref[0, h, :, keys_of(kb)] for h in range(hp)]
        for h, c in chains:
            acc_ref[h, :, qcols(c)] = (alpha_ref[h, :, qcols(c)] * acc_ref[h, :, qcols(c)]
                                       + jnp.dot(vts[h], p_ref[h, :, qcols(c)], preferred_element_type=F32))

    n_full = qi
    d0 = pl.multiple_of(qi * tq, tq)
    sts = [qk(k_ref[0, h, pl.ds(d0, (c + 1) * qc), :], h, c) for h, c in chains]
    tri = (lax.broadcasted_iota(jnp.int32, (qc, qc), 0) <= lax.broadcasted_iota(jnp.int32, (qc, qc), 1))
    for (h, c), st in zip(chains, sts):
        parts = [(c * qc, jnp.where(tri, st[c * qc:, :], -jnp.inf))]
        if c:
            parts.insert(0, (0, st[:c * qc, :]))
        softmax(h, c, parts)
        if (c + 1) * qc < tk:
            p_ref[h, (c + 1) * qc:tk, qcols(c)] = jnp.zeros((tk - (c + 1) * qc, qc), BF16)

    def body(kb, carry):
        ks = [k_ref[0, h, keys_of(kb), :] for h in range(hp)]
        sts = [qk(ks[h], h, c) for h, c in chains]
        pv(jnp.where(kb == 0, n_full, kb - 1))
        for (h, c), st in zip(chains, sts):
            softmax(h, c, [(0, st)])
        return carry

    lax.fori_loop(0, n_full, body, 0)
    pv(jnp.where(n_full == 0, n_full, n_full - 1))

    for h in range(hp):
        acc = acc_ref[h]
        out_t = acc[0:V_DIM, :] / acc[V_DIM:V_DIM + 1, :]
        gate = gate_ref[:, h * V_DIM:(h + 1) * V_DIM].astype(F32)
        o_ref[:, h * V_DIM:(h + 1) * V_DIM] = (out_t.T * _silu(gate)).astype(BF16)


def _attention(qt, k, vt, z, tq, hp):
    batch, heads, _, seq = qt.shape
    nq = seq // tq
    gate_col0 = Z_MLA_GATE // (hp * V_DIM)
    return pl.pallas_call(
        functools.partial(_attn_kernel, tq=tq, qc=min(tq, 256), hp=hp),
        grid=(batch, heads // hp, nq),
        in_specs=[
            pl.BlockSpec((1, hp, QK_DIM, tq), lambda b, h, i: (b, h, 0, i)),
            pl.BlockSpec((1, hp, seq, QK_DIM), lambda b, h, i: (b, h, 0, 0)),
            pl.BlockSpec((1, hp, VT_ROWS, seq), lambda b, h, i: (b, h, 0, 0)),
            pl.BlockSpec((tq, hp * V_DIM), lambda b, h, i: (b * nq + i, gate_col0 + h)),
        ],
        out_specs=pl.BlockSpec((tq, hp * V_DIM), lambda b, h, i: (b * nq + i, h)),
        out_shape=jax.ShapeDtypeStruct((batch * seq, heads * V_DIM), BF16),
        scratch_shapes=[pltpu.VMEM((hp, 1, tq), F32), pltpu.VMEM((hp, 1, tq), F32), pltpu.VMEM((hp, VT_ROWS, tq), F32),
                        pltpu.VMEM((hp, tq, tq), BF16)],
        compiler_params=_params(("parallel", "parallel", "arbitrary")),
        name="mla_attention",
    )(qt, k, vt, z)


HALO = 32
SUBLANES = 8
PITCH = 100
PHASES = 8


def _conv_kernel(uval_ref, ugate_ref, cgate_ref, glub_ref, dww_ref, dwb_ref, lng_ref, lnb_ref, wpw_ref, bpw_ref,
                 o_ref, ubuf_ref, cbuf_ref, carry_ref):
    ts, c = uval_ref.shape
    seg = ts // SUBLANES
    n_slabs = c // LANES
    first = HALO - (CONV_K - 1)

    @pl.when(pl.program_id(1) == 0)
    def _():
        carry_ref[...] = jnp.zeros(carry_ref.shape, F32)

    a = uval_ref[...].astype(F32) + glub_ref[:, 0:c]
    g = ugate_ref[...].astype(F32) + glub_ref[:, c:2 * c]
    u = a * _sigmoid(g)
    for s in range(n_slabs):
        us = u[:, s * LANES:(s + 1) * LANES]
        ubuf_ref[s, first:HALO, :] = carry_ref[s, first:HALO, :]
        for k in range(SUBLANES):
            ubuf_ref[s, k * PITCH + HALO:k * PITCH + HALO + seg, :] = us[k * seg:(k + 1) * seg]
            if k + 1 < SUBLANES:
                ubuf_ref[s, (k + 1) * PITCH + first:(k + 1) * PITCH + HALO, :] = (
                    us[(k + 1) * seg - (HALO - first):(k + 1) * seg])
        carry_ref[s, first:HALO, :] = us[ts - (HALO - first):ts]

    def slab_body(s, carry):
        bias = dwb_ref[s]

        def group_body(gi, carry2):
            b0 = gi * PHASES
            accs = [None] * PHASES
            for r in range(CONV_K - 1 + PHASES):
                rows = ubuf_ref[s, pl.ds(b0 + first + r, SUBLANES, stride=PITCH), :]
                for ph in range(PHASES):
                    j = r - ph
                    if 0 <= j < CONV_K:
                        term = dww_ref[s, j:j + 1, :] * rows
                        accs[ph] = term if accs[ph] is None else accs[ph] + term
            for ph in range(PHASES):
                cbuf_ref[s, pl.ds(b0 + ph, SUBLANES, stride=seg), :] = accs[ph] + bias
            return carry2

        return lax.fori_loop(0, seg // PHASES, group_body, carry, unroll=True)

    lax.fori_loop(0, n_slabs, slab_body, 0)

    y = jnp.concatenate([cbuf_ref[s] for s in range(n_slabs)], axis=-1)
    mu = jnp.mean(y, axis=-1, keepdims=True)
    yc = y - mu
    var = jnp.mean(yc * yc, axis=-1, keepdims=True)
    yn = yc * lax.rsqrt(var + EPS) * lng_ref[...] + lnb_ref[...]
    act = _silu(yn).astype(BF16)
    pw = jnp.dot(act, wpw_ref[...], preferred_element_type=F32) + bpw_ref[...]
    o_ref[...] = (pw * _silu(cgate_ref[...].astype(F32))).astype(BF16)


def _conv_module(z, glub, dww, dwb, lng, lnb, wpw, bpw, batch, seq, ts):
    c = wpw.shape[0]
    n_slabs = c // LANES
    tiles_per_seq = seq // ts
    assert ts % (SUBLANES * PHASES) == 0 and HALO + ts // SUBLANES <= PITCH

    def zcol(offset):
        return pl.BlockSpec((ts, c), lambda b, i: (b * tiles_per_seq + i, offset // c))

    def full(arr):
        return pl.BlockSpec(arr.shape, lambda b, i: (0,) * arr.ndim)

    return pl.pallas_call(
        _conv_kernel,
        grid=(batch, tiles_per_seq),
        in_specs=[zcol(Z_CONV_IN), zcol(Z_CONV_IN + c), zcol(Z_CONV_GATE),
                  full(glub), full(dww), full(dwb), full(lng), full(lnb), full(wpw), full(bpw)],
        out_specs=pl.BlockSpec((ts, c), lambda b, i: (b * tiles_per_seq + i, 0)),
        out_shape=jax.ShapeDtypeStruct((batch * seq, c), BF16),
        scratch_shapes=[pltpu.VMEM((n_slabs, SUBLANES * PITCH, LANES), F32), pltpu.VMEM((n_slabs, ts, LANES), F32),
                        pltpu.VMEM((n_slabs, HALO, LANES), F32)],
        compiler_params=_params(("parallel", "arbitrary")),
        name="conv_module",
    )(z, z, z, glub, dww, dwb, lng, lnb, wpw, bpw)


def _slabs(w):
    rows, c = w.shape
    return w.reshape(rows, c // LANES, LANES).transpose(1, 0, 2)


def _outproj_kernel(mla_ref, conv_ref, x_ref, gate_ref, wa_ref, wb_ref, o_ref):
    y = jnp.dot(mla_ref[...], wa_ref[...], preferred_element_type=F32)
    y = y + jnp.dot(conv_ref[...], wb_ref[...], preferred_element_type=F32)
    o_ref[...] = x_ref[...] + gate_ref[0] * y


def _outproj(mla, conv, x2, mod3, w_a, w_b, seq, tm):
    t, d = x2.shape
    half = mla.shape[1]
    tiles_per_seq = seq // tm
    return pl.pallas_call(
        _outproj_kernel,
        grid=(t // tm,),
        in_specs=[
            pl.BlockSpec((tm, half), lambda i: (i, 0)),
            pl.BlockSpec((tm, half), lambda i: (i, 0)),
            pl.BlockSpec((tm, d), lambda i: (i, 0)),
            pl.BlockSpec((1, 1, d), lambda i: (i // tiles_per_seq, 0, 2)),
            pl.BlockSpec((half, d), lambda i: (0, 0), pipeline_mode=pl.Buffered(1)),
            pl.BlockSpec((half, d), lambda i: (0, 0), pipeline_mode=pl.Buffered(1)),
        ],
        out_specs=pl.BlockSpec((tm, d), lambda i: (i, 0)),
        out_shape=jax.ShapeDtypeStruct((t, d), F32),
        compiler_params=_params(("parallel",)),
        name="outproj",
    )(mla, conv, x2, mod3, w_a, w_b)


def _tile(n, want):
    t = min(n, want)
    assert n % t == 0, (n, t)
    return t


def kernel(x, c, positions, ada_w, ada_b, norm_g, w_in, q_lat_g, w_q_up, kv_lat_g, w_kv_up, q_norm_g, k_norm_g,
           glu_b, dw_w, dw_b, conv_ln_g, conv_ln_b, w_pw, b_pw, w_out):
    batch, seq, d = x.shape
    depth = ada_w.shape[0]
    t = batch * seq
    d_mla = N_HEADS * V_DIM
    d_conv = w_pw.shape[1]
    assert w_in.shape[2] == Q_LORA + KV_LORA + ROPE_DIM + d_mla + 3 * d_conv
    assert d_mla == 1024 and d_conv == 1024 and d == 2048

    inv_freq = 1.0 / (ROPE_THETA ** (jnp.arange(0, ROPE_DIM, 2, dtype=F32) / ROPE_DIM))
    pos_t = jnp.broadcast_to(positions.reshape(1, t), (ROPE_DIM // 2, t))
    inv_t = jnp.broadcast_to(inv_freq[:, None], (ROPE_DIM // 2, _tile(t, 2048)))
    cos_ft, sin_ft = _rope_tables(pos_t, inv_t)

    c_rows = 8
    c_pad = jnp.zeros((c_rows, d), F32).at[:batch].set(c)
    mod = _adaln(c_pad, ada_w, ada_b.reshape(depth, 1, 3 * d), _tile(3 * d, 1024))

    x2 = x.reshape(t, d)
    s0 = Q_LORA
    s1 = s0 + KV_LORA
    s2 = s1 + ROPE_DIM
    s3 = s2 + d_mla
    s4 = s3 + 2 * d_conv
    w_z_t = _wprep(jnp.swapaxes(w_in, 1, 2), ((s3, s4), (s2, s3), (s4, w_in.shape[2]), (0, s0), (s0, s1), (s1, s2)), 256)
    for l in range(depth):
        mod3 = mod[l].reshape(c_rows, 1, 3 * d)
        wqt = w_q_up[l].T.astype(BF16)
        wkv = w_kv_up[l].reshape(KV_LORA, N_HEADS, NOPE_DIM + V_DIM)
        wk = wkv[:, :, :NOPE_DIM].reshape(KV_LORA, N_HEADS * NOPE_DIM).astype(BF16)
        wvt = wkv[:, :, NOPE_DIM:].reshape(KV_LORA, N_HEADS * V_DIM).T.astype(BF16)
        gq, gk = q_norm_g[l], k_norm_g[l]
        gkn = gk[None, :NOPE_DIM]
        gkr = jnp.concatenate([gk[NOPE_DIM:], jnp.zeros((LANES - ROPE_DIM,), F32)])[None, :]

        z = _inproj(x2, mod3, norm_g[l][None, :], w_z_t, l, seq, _tile(seq, 512), Z_COLS)
        tm_qkv = _tile(seq, 512)
        qt, k, vt = _qkv_up(z, cos_ft, sin_ft, q_lat_g[l][None, :], kv_lat_g[l][None, :], wqt, wk, wvt,
                            jnp.broadcast_to(gq[:, None], (QK_DIM, tm_qkv)), gkn, gkr, batch, seq, tm_qkv)
        mla = _attention(qt, k, vt, z, _tile(seq, 1024), 4)
        dww = jnp.concatenate([dw_w[l], jnp.zeros((HALO - CONV_K, d_conv), F32)], axis=0)
        conv = _conv_module(z, glu_b[l][None, :], _slabs(dww), _slabs(dw_b[l][None, :]), conv_ln_g[l][None, :],
                            conv_ln_b[l][None, :], w_pw[l].astype(BF16), b_pw[l][None, :], batch, seq,
                            _tile(seq, 512))
        wo = w_out[l].astype(BF16)
        x2 = _outproj(mla, conv, x2, mod3, wo[:d_mla], wo[d_mla:], seq, _tile(seq, 512))
    return x2.reshape(batch, seq, d)
```

```python
import functools
import math

import jax
import jax.numpy as jnp
from jax import lax
from jax.experimental import pallas as pl
from jax.experimental.pallas import tpu as pltpu

F32 = jnp.float32
BF16 = jnp.bfloat16

N_HEADS = 8
NOPE_DIM = 128
ROPE_DIM = 64
V_DIM = 128
QK_DIM = NOPE_DIM + ROPE_DIM
Q_LORA = 512
KV_LORA = 256
ROPE_THETA = 10000.0
CONV_K = 31
EPS = 1e-6
LANES = 128
BF16_SUBLANES = 16
VT_ROWS = V_DIM + BF16_SUBLANES
_NT = (((1,), (1,)), ((), ()))

Z_CONV_IN = 0
Z_MLA_GATE = 2048
Z_CONV_GATE = 3072
Z_Q_LAT = 4096
Z_KV_LAT = 4608
Z_K_ROPE = 4864
Z_COLS = 5120

VMEM_LIMIT = 56 * 1024 * 1024


def _sigmoid(x):
    return 1.0 / (1.0 + jnp.exp2(x * (-math.log2(math.e))))


def _silu(x):
    return x * _sigmoid(x)


def _params(sem, vmem=VMEM_LIMIT):
    return pltpu.CompilerParams(dimension_semantics=sem, vmem_limit_bytes=vmem)


def _rope_table_kernel(pos_ref, inv_ref, cos_ref, sin_ref):
    ang = pos_ref[...].astype(F32) * inv_ref[...]
    cos_ref[...] = jnp.cos(ang)
    sin_ref[...] = jnp.sin(ang)


def _rope_tables(pos_t, inv_t):
    half, t = pos_t.shape
    tr = inv_t.shape[1]
    col = pl.BlockSpec((half, tr), lambda i: (0, i))
    return pl.pallas_call(
        _rope_table_kernel,
        grid=(t // tr,),
        in_specs=[col, pl.BlockSpec((half, tr), lambda i: (0, 0))],
        out_specs=[col, col],
        out_shape=[jax.ShapeDtypeStruct((half, t), F32)] * 2,
        compiler_params=_params(("parallel",)),
        name="rope_tables",
    )(pos_t, inv_t)


def _adaln_kernel(c_ref, w_ref, b_ref, o_ref):
    c_act = _silu(c_ref[...]).astype(BF16)
    acc = jnp.dot(c_act, w_ref[0].astype(BF16), preferred_element_type=F32)
    o_ref[0] = acc + b_ref[0]


def _adaln(c_pad, ada_w, ada_b3, tn):
    depth, d, n = ada_w.shape
    rows = c_pad.shape[0]
    return pl.pallas_call(
        _adaln_kernel,
        grid=(depth, n // tn),
        in_specs=[
            pl.BlockSpec((rows, d), lambda l, j: (0, 0)),
            pl.BlockSpec((1, d, tn), lambda l, j: (l, 0, j)),
            pl.BlockSpec((1, 1, tn), lambda l, j: (l, 0, j)),
        ],
        out_specs=pl.BlockSpec((1, rows, tn), lambda l, j: (l, 0, j)),
        out_shape=jax.ShapeDtypeStruct((depth, rows, n), F32),
        compiler_params=_params(("parallel", "parallel")),
        name="adaln_mod",
    )(c_pad, ada_w, ada_b3)


def _wprep_kernel(wt_ref, o_ref, *, segments):
    row = 0
    for a, b in segments:
        o_ref[0, row:row + (b - a), :] = wt_ref[0, a:b, :].astype(BF16)
        row += b - a
    o_ref[0, row:, :] = jnp.zeros((o_ref.shape[1] - row, o_ref.shape[2]), BF16)


def _wprep(w_in_t, segments, cb):
    depth, n, d = w_in_t.shape
    return pl.pallas_call(
        functools.partial(_wprep_kernel, segments=segments),
        grid=(depth, d // cb),
        in_specs=[pl.BlockSpec((1, n, cb), lambda l, i: (l, 0, i))],
        out_specs=pl.BlockSpec((1, Z_COLS, cb), lambda l, i: (l, 0, i)),
        out_shape=jax.ShapeDtypeStruct((depth, Z_COLS, d), BF16),
        compiler_params=_params(("parallel", "parallel")),
        name="w_in_layout",
    )(w_in_t)


def _inproj_kernel(x_ref, shift_ref, scale_ref, g_ref, w_ref, z_ref, h0_ref, h1_ref, *, nm, nn):
    i, j = pl.program_id(0), pl.program_id(1)
    tm, d = x_ref.shape
    rows_per_step = tm // nn
    h_refs = (h0_ref, h1_ref)

    def normalise(h_ref):
        rows = pl.ds(pl.multiple_of(j * rows_per_step, rows_per_step), rows_per_step)
        x = x_ref[rows, :]
        inv = lax.rsqrt(jnp.sum(x * x, axis=-1, keepdims=True) * (1.0 / d) + EPS)
        h_ref[rows, :] = (x * inv * (g_ref[...] * (1.0 + scale_ref[0])) + shift_ref[0]).astype(BF16)

    def project(h_ref):
        z_ref[...] = lax.dot_general(h_ref[...], w_ref[0], _NT, preferred_element_type=F32).astype(BF16)

    @pl.when(i == 0)
    def _():
        normalise(h_refs[0])

    for parity in range(2):
        @pl.when(jnp.logical_and(jnp.logical_and(i > 0, i < nm), i % 2 == parity))
        def _():
            project(h_refs[1 - parity])
            normalise(h_refs[parity])

    @pl.when(i == nm)
    def _():
        project(h_refs[(nm - 1) % 2])


def _inproj(x2, mod3, g, w_t, layer, seq, tm, tn):
    t, d = x2.shape
    ncol = w_t.shape[1]
    nm, nn = t // tm, ncol // tn
    tiles_per_seq = seq // tm
    assert tm % nn == 0

    def x_tile(i):
        return jnp.minimum(i, nm - 1)

    def mod_spec(chunk):
        return pl.BlockSpec((1, 1, d), lambda i, j: (x_tile(i) // tiles_per_seq, 0, chunk))

    def w_tile(i, j):
        return jnp.where(i == 0, 0, j)

    return pl.pallas_call(
        functools.partial(_inproj_kernel, nm=nm, nn=nn),
        grid=(nm + 1, nn),
        in_specs=[
            pl.BlockSpec((tm, d), lambda i, j: (x_tile(i), 0)),
            mod_spec(0),
            mod_spec(1),
            pl.BlockSpec((1, d), lambda i, j: (0, 0)),
            pl.BlockSpec((1, tn, d), lambda i, j: (layer, w_tile(i, j), 0),
                         pipeline_mode=pl.Buffered(1) if nn == 1 else None),
        ],
        out_specs=pl.BlockSpec((tm, tn), lambda i, j: (jnp.maximum(i - 1, 0), w_tile(i, j))),
        out_shape=jax.ShapeDtypeStruct((t, ncol), BF16),
        scratch_shapes=[pltpu.VMEM((tm, d), BF16), pltpu.VMEM((tm, d), BF16)],
        compiler_params=_params(("arbitrary", "arbitrary")),
        name="inproj",
    )(x2, mod3, mod3, g, w_t)


def _rms_rows(x, g):
    inv = lax.rsqrt(jnp.mean(x * x, axis=-1, keepdims=True) + EPS)
    return x * inv * g


def _rotate_half_pairs(x):
    lane = lax.broadcasted_iota(jnp.int32, x.shape, 1)
    first = (lane % ROPE_DIM) < (ROPE_DIM // 2)
    return jnp.where(first, pltpu.roll(x, LANES - ROPE_DIM // 2, 1), pltpu.roll(x, ROPE_DIM // 2, 1))


def _qkv_kernel(qlat_ref, kvlat_ref, krope_ref, cost_ref, sint_ref, gql_ref, gkvl_ref,
                wqt_ref, wk_ref, wvt_ref, gq_ref, gkn_ref, gkr_ref, qt_ref, k_ref, vt_ref):
    qn = _rms_rows(qlat_ref[...].astype(F32), gql_ref[...]).astype(BF16)
    kvn = _rms_rows(kvlat_ref[...].astype(F32), gkvl_ref[...]).astype(BF16)
    qt = lax.dot_general(wqt_ref[...], qn, _NT, preferred_element_type=F32)
    kn = jnp.dot(kvn, wk_ref[...], preferred_element_type=F32)
    vt = lax.dot_general(wvt_ref[...], kvn, _NT, preferred_element_type=F32)

    cost = cost_ref[...]
    sint = sint_ref[...]
    gq = gq_ref[...]
    half = ROPE_DIM // 2
    q_scale = math.log2(math.e) / math.sqrt(QK_DIM)
    for h in range(N_HEADS):
        blk = qt[h * QK_DIM:(h + 1) * QK_DIM, :]
        inv = lax.rsqrt(jnp.sum(blk * blk, axis=0, keepdims=True) * (1.0 / QK_DIM) + EPS) * q_scale
        y = blk * gq
        x1 = y[NOPE_DIM:NOPE_DIM + half, :]
        x2 = y[NOPE_DIM + half:QK_DIM, :]
        qt_ref[0, h, 0:NOPE_DIM, :] = (y[0:NOPE_DIM, :] * inv).astype(BF16)
        qt_ref[0, h, NOPE_DIM:NOPE_DIM + half, :] = ((x1 * cost - x2 * sint) * inv).astype(BF16)
        qt_ref[0, h, NOPE_DIM + half:QK_DIM, :] = ((x2 * cost + x1 * sint) * inv).astype(BF16)

    kr = krope_ref[...].astype(F32)
    kr_ss = jnp.sum(kr * kr, axis=-1, keepdims=True)
    krg = kr * gkr_ref[...]
    cos_tok = jnp.concatenate([cost] * (LANES // half), axis=0).T
    sin_tok = jnp.concatenate([-sint, sint] * (LANES // ROPE_DIM), axis=0).T
    kr_rot = krg * cos_tok + _rotate_half_pairs(krg) * sin_tok
    ones_rows = jnp.ones((VT_ROWS - V_DIM, vt.shape[1]), BF16)
    for h in range(N_HEADS):
        k_nope = kn[:, h * NOPE_DIM:(h + 1) * NOPE_DIM]
        ss = jnp.sum(k_nope * k_nope, axis=-1, keepdims=True) + kr_ss
        inv = lax.rsqrt(ss * (1.0 / QK_DIM) + EPS)
        k_ref[0, h, :, 0:NOPE_DIM] = (k_nope * inv * gkn_ref[...]).astype(BF16)
        k_ref[0, h, :, NOPE_DIM:QK_DIM] = (kr_rot[:, 0:ROPE_DIM] * inv).astype(BF16)
        vt_ref[0, h, 0:V_DIM, :] = vt[h * V_DIM:(h + 1) * V_DIM, :].astype(BF16)
        vt_ref[0, h, V_DIM:VT_ROWS, :] = ones_rows


def _qkv_up(z, cos_ft, sin_ft, gql, gkvl, wqt, wk, wvt, gq, gkn, gkr, batch, seq, tm):
    tiles_per_seq = seq // tm

    def zcol(width, offset):
        return pl.BlockSpec((tm, width), lambda i: (i, offset // width))

    def full(arr):
        return pl.BlockSpec(arr.shape, lambda i: (0,) * arr.ndim)

    def feat_major(rows):
        return pl.BlockSpec((1, N_HEADS, rows, tm), lambda i: (i // tiles_per_seq, 0, 0, i % tiles_per_seq))

    k_out = pl.BlockSpec((1, N_HEADS, tm, QK_DIM), lambda i: (i // tiles_per_seq, 0, i % tiles_per_seq, 0))
    col = pl.BlockSpec((ROPE_DIM // 2, tm), lambda i: (0, i))
    return pl.pallas_call(
        _qkv_kernel,
        grid=(batch * tiles_per_seq,),
        in_specs=[zcol(Q_LORA, Z_Q_LAT), zcol(KV_LORA, Z_KV_LAT), zcol(LANES, Z_K_ROPE), col, col,
                  full(gql), full(gkvl), full(wqt), full(wk), full(wvt), full(gq), full(gkn), full(gkr)],
        out_specs=[feat_major(QK_DIM), k_out, feat_major(VT_ROWS)],
        out_shape=[jax.ShapeDtypeStruct((batch, N_HEADS, QK_DIM, seq), BF16),
                   jax.ShapeDtypeStruct((batch, N_HEADS, seq, QK_DIM), BF16),
                   jax.ShapeDtypeStruct((batch, N_HEADS, VT_ROWS, seq), BF16)],
        compiler_params=_params(("parallel",)),
        name="qkv_up",
    )(z, z, z, cos_ft, sin_ft, gql, gkvl, wqt, wk, wvt, gq, gkn, gkr)


def _attn_kernel(qt_ref, k_ref, vt_ref, gate_ref, o_ref, m_ref, alpha_ref, acc_ref, p_ref, *, tq, qc, hp):
    tk = tq
    qi = pl.program_id(2)
    m_ref[...] = jnp.full(m_ref.shape, -jnp.inf, F32)
    acc_ref[...] = jnp.zeros(acc_ref.shape, F32)
    chains = tuple((h, c) for h in range(hp) for c in range(tq // qc))

    def qcols(c):
        return slice(c * qc, (c + 1) * qc)

    def keys_of(kb):
        return pl.ds(pl.multiple_of(kb * tk, tk), tk)

    def qk(k, h, c):
        return jnp.dot(k, qt_ref[0, h, :, qcols(c)], preferred_element_type=F32)

    def softmax(h, c, parts):
        m_prev = m_ref[h, :, qcols(c)]
        m_new = m_prev
        for _, st in parts:
            m_new = jnp.maximum(m_new, jnp.max(st, axis=0, keepdims=True))
        alpha_ref[h, :, qcols(c)] = jnp.exp2(m_prev - m_new)
        for r0, st in parts:
            p_ref[h, r0:r0 + st.shape[0], qcols(c)] = jnp.exp2(st - m_new).astype(BF16)
        m_ref[h, :, qcols(c)] = m_new

    def pv(kb):
        vts = [vt_ref[0, h, :, keys_of(kb)] for h in range(hp)]
        for h, c in chains:
            acc_ref[h, :, qcols(c)] = (alpha_ref[h, :, qcols(c)] * acc_ref[h, :, qcols(c)]
                                       + jnp.dot(vts[h], p_ref[h, :, qcols(c)], preferred_element_type=F32))

    n_full = qi
    d0 = pl.multiple_of(qi * tq, tq)
    sts = [qk(k_ref[0, h, pl.ds(d0, (c + 1) * qc), :], h, c) for h, c in chains]
    tri = (lax.broadcasted_iota(jnp.int32, (qc, qc), 0) <= lax.broadcasted_iota(jnp.int32, (qc, qc), 1))
    for (h, c), st in zip(chains, sts):
        parts = [(c * qc, jnp.where(tri, st[c * qc:, :], -jnp.inf))]
        if c:
            parts.insert(0, (0, st[:c * qc, :]))
        softmax(h, c, parts)
        if (c + 1) * qc < tk:
            p_ref[h, (c + 1) * qc:tk, qcols(c)] = jnp.zeros((tk - (c + 1) * qc, qc), BF16)

    def body(kb, carry):
        ks = [k_ref[0, h, keys_of(kb), :] for h in range(hp)]
        sts = [qk(ks[h], h, c) for h, c in chains]
        pv(jnp.where(kb == 0, n_full, kb - 1))
        for (h, c), st in zip(chains, sts):
            softmax(h, c, [(0, st)])
        return carry

    lax.fori_loop(0, n_full, body, 0)
    pv(jnp.where(n_full == 0, n_full, n_full - 1))

    for h in range(hp):
        acc = acc_ref[h]
        out_t = acc[0:V_DIM, :] / acc[V_DIM:V_DIM + 1, :]
        gate = gate_ref[:, h * V_DIM:(h + 1) * V_DIM].astype(F32)
        o_ref[:, h * V_DIM:(h + 1) * V_DIM] = (out_t.T * _silu(gate)).astype(BF16)


def _attention(qt, k, vt, z, tq, hp):
    batch, heads, _, seq = qt.shape
    nq = seq // tq
    gate_col0 = Z_MLA_GATE // (hp * V_DIM)
    return pl.pallas_call(
        functools.partial(_attn_kernel, tq=tq, qc=min(tq, 256), hp=hp),
        grid=(batch, heads // hp, nq),
        in_specs=[
            pl.BlockSpec((1, hp, QK_DIM, tq), lambda b, h, i: (b, h, 0, i)),
            pl.BlockSpec((1, hp, seq, QK_DIM), lambda b, h, i: (b, h, 0, 0)),
            pl.BlockSpec((1, hp, VT_ROWS, seq), lambda b, h, i: (b, h, 0, 0)),
            pl.BlockSpec((tq, hp * V_DIM), lambda b, h, i: (b * nq + i, gate_col0 + h)),
        ],
        out_specs=pl.BlockSpec((tq, hp * V_DIM), lambda b, h, i: (b * nq + i, h)),
        out_shape=jax.ShapeDtypeStruct((batch * seq, heads * V_DIM), BF16),
        scratch_shapes=[pltpu.VMEM((hp, 1, tq), F32), pltpu.VMEM((hp, 1, tq), F32), pltpu.VMEM((hp, VT_ROWS, tq), F32),
                        pltpu.VMEM((hp, tq, tq), BF16)],
        compiler_params=_params(("parallel", "parallel", "arbitrary")),
        name="mla_attention",
    )(qt, k, vt, z)


HALO = 32
SUBLANES = 8
PITCH = 164
PHASES = 8


def _conv_kernel(uval_ref, ugate_ref, cgate_ref, glub_ref, dww_ref, dwb_ref, lng_ref, lnb_ref, wpw_ref, bpw_ref,
                 o_ref, ubuf_ref, cbuf_ref, carry_ref):
    ts, c = uval_ref.shape
    seg = ts // SUBLANES
    n_slabs = c // LANES
    first = HALO - (CONV_K - 1)

    @pl.when(pl.program_id(1) == 0)
    def _():
        carry_ref[...] = jnp.zeros(carry_ref.shape, F32)

    a = uval_ref[...].astype(F32) + glub_ref[:, 0:c]
    g = ugate_ref[...].astype(F32) + glub_ref[:, c:2 * c]
    u = a * _sigmoid(g)
    for s in range(n_slabs):
        us = u[:, s * LANES:(s + 1) * LANES]
        ubuf_ref[s, first:HALO, :] = carry_ref[s, first:HALO, :]
        for k in range(SUBLANES):
            ubuf_ref[s, k * PITCH + HALO:k * PITCH + HALO + seg, :] = us[k * seg:(k + 1) * seg]
            if k + 1 < SUBLANES:
                ubuf_ref[s, (k + 1) * PITCH + first:(k + 1) * PITCH + HALO, :] = (
                    us[(k + 1) * seg - (HALO - first):(k + 1) * seg])
        carry_ref[s, first:HALO, :] = us[ts - (HALO - first):ts]

    def slab_body(s, carry):
        bias = dwb_ref[s]

        def group_body(gi, carry2):
            b0 = gi * PHASES
            accs = [None] * PHASES
            for r in range(CONV_K - 1 + PHASES):
                rows = ubuf_ref[s, pl.ds(b0 + first + r, SUBLANES, stride=PITCH), :]
                for ph in range(PHASES):
                    j = r - ph
                    if 0 <= j < CONV_K:
                        term = dww_ref[s, j:j + 1, :] * rows
                        accs[ph] = term if accs[ph] is None else accs[ph] + term
            for ph in range(PHASES):
                cbuf_ref[s, pl.ds(b0 + ph, SUBLANES, stride=seg), :] = accs[ph] + bias
            return carry2

        return lax.fori_loop(0, seg // PHASES, group_body, carry, unroll=True)

    lax.fori_loop(0, n_slabs, slab_body, 0)

    y = jnp.concatenate([cbuf_ref[s] for s in range(n_slabs)], axis=-1)
    mu = jnp.mean(y, axis=-1, keepdims=True)
    yc = y - mu
    var = jnp.mean(yc * yc, axis=-1, keepdims=True)
    yn = yc * lax.rsqrt(var + EPS) * lng_ref[...] + lnb_ref[...]
    act = _silu(yn).astype(BF16)
    pw = jnp.dot(act, wpw_ref[...], preferred_element_type=F32) + bpw_ref[...]
    o_ref[...] = (pw * _silu(cgate_ref[...].astype(F32))).astype(BF16)


def _conv_module(z, glub, dww, dwb, lng, lnb, wpw, bpw, batch, seq, ts):
    c = wpw.shape[0]
    n_slabs = c // LANES
    tiles_per_seq = seq // ts
    assert ts % (SUBLANES * PHASES) == 0 and HALO + ts // SUBLANES <= PITCH

    def zcol(offset):
        return pl.BlockSpec((ts, c), lambda b, i: (b * tiles_per_seq + i, offset // c))

    def full(arr):
        return pl.BlockSpec(arr.shape, lambda b, i: (0,) * arr.ndim)

    return pl.pallas_call(
        _conv_kernel,
        grid=(batch, tiles_per_seq),
        in_specs=[zcol(Z_CONV_IN), zcol(Z_CONV_IN + c), zcol(Z_CONV_GATE),
                  full(glub), full(dww), full(dwb), full(lng), full(lnb), full(wpw), full(bpw)],
        out_specs=pl.BlockSpec((ts, c), lambda b, i: (b * tiles_per_seq + i, 0)),
        out_shape=jax.ShapeDtypeStruct((batch * seq, c), BF16),
        scratch_shapes=[pltpu.VMEM((n_slabs, SUBLANES * PITCH, LANES), F32), pltpu.VMEM((n_slabs, ts, LANES), F32),
                        pltpu.VMEM((n_slabs, HALO, LANES), F32)],
        compiler_params=_params(("parallel", "arbitrary")),
        name="conv_module",
    )(z, z, z, glub, dww, dwb, lng, lnb, wpw, bpw)


def _slabs(w):
    rows, c = w.shape
    return w.reshape(rows, c // LANES, LANES).transpose(1, 0, 2)


def _outproj_kernel(mla_ref, conv_ref, x_ref, gate_ref, wa_ref, wb_ref, o_ref):
    y = jnp.dot(mla_ref[...], wa_ref[...], preferred_element_type=F32)
    y = y + jnp.dot(conv_ref[...], wb_ref[...], preferred_element_type=F32)
    o_ref[...] = x_ref[...] + gate_ref[0] * y


def _outproj(mla, conv, x2, mod3, w_a, w_b, seq, tm):
    t, d = x2.shape
    half = mla.shape[1]
    tiles_per_seq = seq // tm
    return pl.pallas_call(
        _outproj_kernel,
        grid=(t // tm,),
        in_specs=[
            pl.BlockSpec((tm, half), lambda i: (i, 0)),
            pl.BlockSpec((tm, half), lambda i: (i, 0)),
            pl.BlockSpec((tm, d), lambda i: (i, 0)),
            pl.BlockSpec((1, 1, d), lambda i: (i // tiles_per_seq, 0, 2)),
            pl.BlockSpec((half, d), lambda i: (0, 0), pipeline_mode=pl.Buffered(1)),
            pl.BlockSpec((half, d), lambda i: (0, 0), pipeline_mode=pl.Buffered(1)),
        ],
        out_specs=pl.BlockSpec((tm, d), lambda i: (i, 0)),
        out_shape=jax.ShapeDtypeStruct((t, d), F32),
        compiler_params=_params(("parallel",)),
        name="outproj",
    )(mla, conv, x2, mod3, w_a, w_b)


def _tile(n, want):
    t = min(n, want)
    assert n % t == 0, (n, t)
    return t


def kernel(x, c, positions, ada_w, ada_b, norm_g, w_in, q_lat_g, w_q_up, kv_lat_g, w_kv_up, q_norm_g, k_norm_g,
           glu_b, dw_w, dw_b, conv_ln_g, conv_ln_b, w_pw, b_pw, w_out):
    batch, seq, d = x.shape
    depth = ada_w.shape[0]
    t = batch * seq
    d_mla = N_HEADS * V_DIM
    d_conv = w_pw.shape[1]
    assert w_in.shape[2] == Q_LORA + KV_LORA + ROPE_DIM + d_mla + 3 * d_conv
    assert d_mla == 1024 and d_conv == 1024 and d == 2048

    inv_freq = 1.0 / (ROPE_THETA ** (jnp.arange(0, ROPE_DIM, 2, dtype=F32) / ROPE_DIM))
    pos_t = jnp.broadcast_to(positions.reshape(1, t), (ROPE_DIM // 2, t))
    inv_t = jnp.broadcast_to(inv_freq[:, None], (ROPE_DIM // 2, _tile(t, 2048)))
    cos_ft, sin_ft = _rope_tables(pos_t, inv_t)

    c_rows = 8
    c_pad = jnp.zeros((c_rows, d), F32).at[:batch].set(c)
    mod = _adaln(c_pad, ada_w, ada_b.reshape(depth, 1, 3 * d), _tile(3 * d, 1024))

    x2 = x.reshape(t, d)
    s0 = Q_LORA
    s1 = s0 + KV_LORA
    s2 = s1 + ROPE_DIM
    s3 = s2 + d_mla
    s4 = s3 + 2 * d_conv
    w_z_t = _wprep(jnp.swapaxes(w_in, 1, 2), ((s3, s4), (s2, s3), (s4, w_in.shape[2]), (0, s0), (s0, s1), (s1, s2)), 256)
    for l in range(depth):
        mod3 = mod[l].reshape(c_rows, 1, 3 * d)
        wqt = w_q_up[l].T.astype(BF16)
        wkv = w_kv_up[l].reshape(KV_LORA, N_HEADS, NOPE_DIM + V_DIM)
        wk = wkv[:, :, :NOPE_DIM].reshape(KV_LORA, N_HEADS * NOPE_DIM).astype(BF16)
        wvt = wkv[:, :, NOPE_DIM:].reshape(KV_LORA, N_HEADS * V_DIM).T.astype(BF16)
        gq, gk = q_norm_g[l], k_norm_g[l]
        gkn = gk[None, :NOPE_DIM]
        gkr = jnp.concatenate([gk[NOPE_DIM:], jnp.zeros((LANES - ROPE_DIM,), F32)])[None, :]

        z = _inproj(x2, mod3, norm_g[l][None, :], w_z_t, l, seq, _tile(seq, 512), Z_COLS)
        tm_qkv = _tile(seq, 1024)
        qt, k, vt = _qkv_up(z, cos_ft, sin_ft, q_lat_g[l][None, :], kv_lat_g[l][None, :], wqt, wk, wvt,
                            jnp.broadcast_to(gq[:, None], (QK_DIM, tm_qkv)), gkn, gkr, batch, seq, tm_qkv)
        mla = _attention(qt, k, vt, z, _tile(seq, 1024), 4)
        dww = jnp.concatenate([dw_w[l], jnp.zeros((HALO - CONV_K, d_conv), F32)], axis=0)
        conv = _conv_module(z, glu_b[l][None, :], _slabs(dww), _slabs(dw_b[l][None, :]), conv_ln_g[l][None, :],
                            conv_ln_b[l][None, :], w_pw[l].astype(BF16), b_pw[l][None, :], batch, seq,
                            _tile(seq, 1024))
        wo = w_out[l].astype(BF16)
        x2 = _outproj(mla, conv, x2, mod3, wo[:d_mla], wo[d_mla:], seq, _tile(seq, 512))
    return x2.reshape(batch, seq, d)
```

```python
import functools
import math

import jax
import jax.numpy as jnp
from jax import lax
from jax.experimental import pallas as pl
from jax.experimental.pallas import tpu as pltpu

F32 = jnp.float32
BF16 = jnp.bfloat16

N_HEADS = 8
NOPE_DIM = 128
ROPE_DIM = 64
V_DIM = 128
QK_DIM = NOPE_DIM + ROPE_DIM
Q_LORA = 512
KV_LORA = 256
ROPE_THETA = 10000.0
CONV_K = 31
EPS = 1e-6
LANES = 128
BF16_SUBLANES = 16
VT_ROWS = V_DIM + BF16_SUBLANES
_NT = (((1,), (1,)), ((), ()))

Z_CONV_IN = 0
Z_MLA_GATE = 2048
Z_CONV_GATE = 3072
Z_Q_LAT = 4096
Z_KV_LAT = 4608
Z_K_ROPE = 4864
Z_COLS = 5120

VMEM_LIMIT = 56 * 1024 * 1024

ROPE_TABLE_COLS = 2048
ADALN_COLS = 1024
WPREP_COLS = 256
INPROJ_ROWS = 512
QKV_ROWS = 1024
ATTN_QUERY_TILE = 1024
ATTN_HEADS_PER_STEP = 4
ATTN_CHAIN_COLS = 256
CONV_ROWS = 1024
OUTPROJ_ROWS = 512


def _sigmoid(x):
    return 1.0 / (1.0 + jnp.exp2(x * (-math.log2(math.e))))


def _silu(x):
    return x * _sigmoid(x)


def _params(sem, vmem=VMEM_LIMIT):
    return pltpu.CompilerParams(dimension_semantics=sem, vmem_limit_bytes=vmem)


def _rope_table_kernel(pos_ref, inv_ref, cos_ref, sin_ref):
    ang = pos_ref[...].astype(F32) * inv_ref[...]
    cos_ref[...] = jnp.cos(ang)
    sin_ref[...] = jnp.sin(ang)


def _rope_tables(pos_t, inv_t):
    half, t = pos_t.shape
    tr = inv_t.shape[1]
    col = pl.BlockSpec((half, tr), lambda i: (0, i))
    return pl.pallas_call(
        _rope_table_kernel,
        grid=(t // tr,),
        in_specs=[col, pl.BlockSpec((half, tr), lambda i: (0, 0))],
        out_specs=[col, col],
        out_shape=[jax.ShapeDtypeStruct((half, t), F32)] * 2,
        compiler_params=_params(("parallel",)),
        name="rope_tables",
    )(pos_t, inv_t)


def _adaln_kernel(c_ref, w_ref, b_ref, o_ref):
    c_act = _silu(c_ref[...]).astype(BF16)
    acc = jnp.dot(c_act, w_ref[0].astype(BF16), preferred_element_type=F32)
    o_ref[0] = acc + b_ref[0]


def _adaln(c_pad, ada_w, ada_b3, tn):
    depth, d, n = ada_w.shape
    rows = c_pad.shape[0]
    return pl.pallas_call(
        _adaln_kernel,
        grid=(depth, n // tn),
        in_specs=[
            pl.BlockSpec((rows, d), lambda l, j: (0, 0)),
            pl.BlockSpec((1, d, tn), lambda l, j: (l, 0, j)),
            pl.BlockSpec((1, 1, tn), lambda l, j: (l, 0, j)),
        ],
        out_specs=pl.BlockSpec((1, rows, tn), lambda l, j: (l, 0, j)),
        out_shape=jax.ShapeDtypeStruct((depth, rows, n), F32),
        compiler_params=_params(("parallel", "parallel")),
        name="adaln_mod",
    )(c_pad, ada_w, ada_b3)


def _wprep_kernel(wt_ref, o_ref, *, segments):
    row = 0
    for a, b in segments:
        o_ref[0, row:row + (b - a), :] = wt_ref[0, a:b, :].astype(BF16)
        row += b - a
    o_ref[0, row:, :] = jnp.zeros((o_ref.shape[1] - row, o_ref.shape[2]), BF16)


def _wprep(w_in_t, segments, cb):
    depth, n, d = w_in_t.shape
    return pl.pallas_call(
        functools.partial(_wprep_kernel, segments=segments),
        grid=(depth, d // cb),
        in_specs=[pl.BlockSpec((1, n, cb), lambda l, i: (l, 0, i))],
        out_specs=pl.BlockSpec((1, Z_COLS, cb), lambda l, i: (l, 0, i)),
        out_shape=jax.ShapeDtypeStruct((depth, Z_COLS, d), BF16),
        compiler_params=_params(("parallel", "parallel")),
        name="w_in_layout",
    )(w_in_t)


def _inproj_kernel(x_ref, shift_ref, scale_ref, g_ref, w_ref, z_ref, h0_ref, h1_ref, *, nm, nn):
    i, j = pl.program_id(0), pl.program_id(1)
    tm, d = x_ref.shape
    rows_per_step = tm // nn
    h_refs = (h0_ref, h1_ref)

    def normalise(h_ref):
        rows = pl.ds(pl.multiple_of(j * rows_per_step, rows_per_step), rows_per_step)
        x = x_ref[rows, :]
        inv = lax.rsqrt(jnp.sum(x * x, axis=-1, keepdims=True) * (1.0 / d) + EPS)
        h_ref[rows, :] = (x * inv * (g_ref[...] * (1.0 + scale_ref[0])) + shift_ref[0]).astype(BF16)

    def project(h_ref):
        z_ref[...] = lax.dot_general(h_ref[...], w_ref[0], _NT, preferred_element_type=F32).astype(BF16)

    @pl.when(i == 0)
    def _():
        normalise(h_refs[0])

    for parity in range(2):
        @pl.when(jnp.logical_and(jnp.logical_and(i > 0, i < nm), i % 2 == parity))
        def _():
            project(h_refs[1 - parity])
            normalise(h_refs[parity])

    @pl.when(i == nm)
    def _():
        project(h_refs[(nm - 1) % 2])


def _inproj(x2, mod3, g, w_t, layer, seq, tm, tn):
    t, d = x2.shape
    ncol = w_t.shape[1]
    nm, nn = t // tm, ncol // tn
    tiles_per_seq = seq // tm
    assert tm % nn == 0

    def x_tile(i):
        return jnp.minimum(i, nm - 1)

    def mod_spec(chunk):
        return pl.BlockSpec((1, 1, d), lambda i, j: (x_tile(i) // tiles_per_seq, 0, chunk))

    def w_tile(i, j):
        return jnp.where(i == 0, 0, j)

    return pl.pallas_call(
        functools.partial(_inproj_kernel, nm=nm, nn=nn),
        grid=(nm + 1, nn),
        in_specs=[
            pl.BlockSpec((tm, d), lambda i, j: (x_tile(i), 0)),
            mod_spec(0),
            mod_spec(1),
            pl.BlockSpec((1, d), lambda i, j: (0, 0)),
            pl.BlockSpec((1, tn, d), lambda i, j: (layer, w_tile(i, j), 0),
                         pipeline_mode=pl.Buffered(1) if nn == 1 else None),
        ],
        out_specs=pl.BlockSpec((tm, tn), lambda i, j: (jnp.maximum(i - 1, 0), w_tile(i, j))),
        out_shape=jax.ShapeDtypeStruct((t, ncol), BF16),
        scratch_shapes=[pltpu.VMEM((tm, d), BF16), pltpu.VMEM((tm, d), BF16)],
        compiler_params=_params(("arbitrary", "arbitrary")),
        name="inproj",
    )(x2, mod3, mod3, g, w_t)


def _rms_rows(x, g):
    inv = lax.rsqrt(jnp.mean(x * x, axis=-1, keepdims=True) + EPS)
    return x * inv * g


def _rotate_half_pairs(x):
    lane = lax.broadcasted_iota(jnp.int32, x.shape, 1)
    first = (lane % ROPE_DIM) < (ROPE_DIM // 2)
    return jnp.where(first, pltpu.roll(x, LANES - ROPE_DIM // 2, 1), pltpu.roll(x, ROPE_DIM // 2, 1))


def _qkv_kernel(qlat_ref, kvlat_ref, krope_ref, cost_ref, sint_ref, gql_ref, gkvl_ref,
                wqt_ref, wk_ref, wvt_ref, gq_ref, gkn_ref, gkr_ref, qt_ref, k_ref, vt_ref):
    qn = _rms_rows(qlat_ref[...].astype(F32), gql_ref[...]).astype(BF16)
    kvn = _rms_rows(kvlat_ref[...].astype(F32), gkvl_ref[...]).astype(BF16)
    qt = lax.dot_general(wqt_ref[...], qn, _NT, preferred_element_type=F32)
    kn = jnp.dot(kvn, wk_ref[...], preferred_element_type=F32)
    vt = lax.dot_general(wvt_ref[...], kvn, _NT, preferred_element_type=F32)

    cost = cost_ref[...]
    sint = sint_ref[...]
    gq = gq_ref[...]
    half = ROPE_DIM // 2
    q_scale = math.log2(math.e) / math.sqrt(QK_DIM)
    for h in range(N_HEADS):
        blk = qt[h * QK_DIM:(h + 1) * QK_DIM, :]
        inv = lax.rsqrt(jnp.sum(blk * blk, axis=0, keepdims=True) * (1.0 / QK_DIM) + EPS) * q_scale
        y = blk * gq
        x1 = y[NOPE_DIM:NOPE_DIM + half, :]
        x2 = y[NOPE_DIM + half:QK_DIM, :]
        qt_ref[0, h, 0:NOPE_DIM, :] = (y[0:NOPE_DIM, :] * inv).astype(BF16)
        qt_ref[0, h, NOPE_DIM:NOPE_DIM + half, :] = ((x1 * cost - x2 * sint) * inv).astype(BF16)
        qt_ref[0, h, NOPE_DIM + half:QK_DIM, :] = ((x2 * cost + x1 * sint) * inv).astype(BF16)

    kr = krope_ref[...].astype(F32)
    kr_ss = jnp.sum(kr * kr, axis=-1, keepdims=True)
    krg = kr * gkr_ref[...]
    cos_tok = jnp.concatenate([cost] * (LANES // half), axis=0).T
    sin_tok = jnp.concatenate([-sint, sint] * (LANES // ROPE_DIM), axis=0).T
    kr_rot = krg * cos_tok + _rotate_half_pairs(krg) * sin_tok
    ones_rows = jnp.ones((VT_ROWS - V_DIM, vt.shape[1]), BF16)
    for h in range(N_HEADS):
        k_nope = kn[:, h * NOPE_DIM:(h + 1) * NOPE_DIM]
        ss = jnp.sum(k_nope * k_nope, axis=-1, keepdims=True) + kr_ss
        inv = lax.rsqrt(ss * (1.0 / QK_DIM) + EPS)
        k_ref[0, h, :, 0:NOPE_DIM] = (k_nope * inv * gkn_ref[...]).astype(BF16)
        k_ref[0, h, :, NOPE_DIM:QK_DIM] = (kr_rot[:, 0:ROPE_DIM] * inv).astype(BF16)
        vt_ref[0, h, 0:V_DIM, :] = vt[h * V_DIM:(h + 1) * V_DIM, :].astype(BF16)
        vt_ref[0, h, V_DIM:VT_ROWS, :] = ones_rows


def _qkv_up(z, cos_ft, sin_ft, gql, gkvl, wqt, wk, wvt, gq, gkn, gkr, batch, seq, tm):
    tiles_per_seq = seq // tm

    def zcol(width, offset):
        return pl.BlockSpec((tm, width), lambda i: (i, offset // width))

    def full(arr):
        return pl.BlockSpec(arr.shape, lambda i: (0,) * arr.ndim)

    def feat_major(rows):
        return pl.BlockSpec((1, N_HEADS, rows, tm), lambda i: (i // tiles_per_seq, 0, 0, i % tiles_per_seq))

    k_out = pl.BlockSpec((1, N_HEADS, tm, QK_DIM), lambda i: (i // tiles_per_seq, 0, i % tiles_per_seq, 0))
    col = pl.BlockSpec((ROPE_DIM // 2, tm), lambda i: (0, i))
    return pl.pallas_call(
        _qkv_kernel,
        grid=(batch * tiles_per_seq,),
        in_specs=[zcol(Q_LORA, Z_Q_LAT), zcol(KV_LORA, Z_KV_LAT), zcol(LANES, Z_K_ROPE), col, col,
                  full(gql), full(gkvl), full(wqt), full(wk), full(wvt), full(gq), full(gkn), full(gkr)],
        out_specs=[feat_major(QK_DIM), k_out, feat_major(VT_ROWS)],
        out_shape=[jax.ShapeDtypeStruct((batch, N_HEADS, QK_DIM, seq), BF16),
                   jax.ShapeDtypeStruct((batch, N_HEADS, seq, QK_DIM), BF16),
                   jax.ShapeDtypeStruct((batch, N_HEADS, VT_ROWS, seq), BF16)],
        compiler_params=_params(("parallel",)),
        name="qkv_up",
    )(z, z, z, cos_ft, sin_ft, gql, gkvl, wqt, wk, wvt, gq, gkn, gkr)


def _attn_kernel(qt_ref, k_ref, vt_ref, gate_ref, o_ref, m_ref, alpha_ref, acc_ref, p_ref, *, tq, qc, hp):
    tk = tq
    qi = pl.program_id(2)
    m_ref[...] = jnp.full(m_ref.shape, -jnp.inf, F32)
    acc_ref[...] = jnp.zeros(acc_ref.shape, F32)
    chains = tuple((h, c) for h in range(hp) for c in range(tq // qc))

    def qcols(c):
        return slice(c * qc, (c + 1) * qc)

    def keys_of(kb):
        return pl.ds(pl.multiple_of(kb * tk, tk), tk)

    def qk(k, h, c):
        return jnp.dot(k, qt_ref[0, h, :, qcols(c)], preferred_element_type=F32)

    def softmax(h, c, parts):
        m_prev = m_ref[h, :, qcols(c)]
        m_new = m_prev
        for _, st in parts:
            m_new = jnp.maximum(m_new, jnp.max(st, axis=0, keepdims=True))
        alpha_ref[h, :, qcols(c)] = jnp.exp2(m_prev - m_new)
        for r0, st in parts:
            p_ref[h, r0:r0 + st.shape[0], qcols(c)] = jnp.exp2(st - m_new).astype(BF16)
        m_ref[h, :, qcols(c)] = m_new

    def pv(kb):
        vts = [vt_ref[0, h, :, keys_of(kb)] for h in range(hp)]
        for h, c in chains:
            acc_ref[h, :, qcols(c)] = (alpha_ref[h, :, qcols(c)] * acc_ref[h, :, qcols(c)]
                                       + jnp.dot(vts[h], p_ref[h, :, qcols(c)], preferred_element_type=F32))

    n_full = qi
    d0 = pl.multiple_of(qi * tq, tq)
    sts = [qk(k_ref[0, h, pl.ds(d0, (c + 1) * qc), :], h, c) for h, c in chains]
    tri = (lax.broadcasted_iota(jnp.int32, (qc, qc), 0) <= lax.broadcasted_iota(jnp.int32, (qc, qc), 1))
    for (h, c), st in zip(chains, sts):
        parts = [(c * qc, jnp.where(tri, st[c * qc:, :], -jnp.inf))]
        if c:
            parts.insert(0, (0, st[:c * qc, :]))
        softmax(h, c, parts)
        if (c + 1) * qc < tk:
            p_ref[h, (c + 1) * qc:tk, qcols(c)] = jnp.zeros((tk - (c + 1) * qc, qc), BF16)

    def body(kb, carry):
        ks = [k_ref[0, h, keys_of(kb), :] for h in range(hp)]
        sts = [qk(ks[h], h, c) for h, c in chains]
        pv(jnp.where(kb == 0, n_full, kb - 1))
        for (h, c), st in zip(chains, sts):
            softmax(h, c, [(0, st)])
        return carry

    lax.fori_loop(0, n_full, body, 0)
    pv(jnp.where(n_full == 0, n_full, n_full - 1))

    for h in range(hp):
        acc = acc_ref[h]
        out_t = acc[0:V_DIM, :] / acc[V_DIM:V_DIM + 1, :]
        gate = gate_ref[:, h * V_DIM:(h + 1) * V_DIM].astype(F32)
        o_ref[:, h * V_DIM:(h + 1) * V_DIM] = (out_t.T * _silu(gate)).astype(BF16)


def _attention(qt, k, vt, z, tq, hp):
    batch, heads, _, seq = qt.shape
    nq = seq // tq
    gate_col0 = Z_MLA_GATE // (hp * V_DIM)
    return pl.pallas_call(
        functools.partial(_attn_kernel, tq=tq, qc=min(tq, ATTN_CHAIN_COLS), hp=hp),
        grid=(batch, heads // hp, nq),
        in_specs=[
            pl.BlockSpec((1, hp, QK_DIM, tq), lambda b, h, i: (b, h, 0, i)),
            pl.BlockSpec((1, hp, seq, QK_DIM), lambda b, h, i: (b, h, 0, 0)),
            pl.BlockSpec((1, hp, VT_ROWS, seq), lambda b, h, i: (b, h, 0, 0)),
            pl.BlockSpec((tq, hp * V_DIM), lambda b, h, i: (b * nq + i, gate_col0 + h)),
        ],
        out_specs=pl.BlockSpec((tq, hp * V_DIM), lambda b, h, i: (b * nq + i, h)),
        out_shape=jax.ShapeDtypeStruct((batch * seq, heads * V_DIM), BF16),
        scratch_shapes=[pltpu.VMEM((hp, 1, tq), F32), pltpu.VMEM((hp, 1, tq), F32), pltpu.VMEM((hp, VT_ROWS, tq), F32),
                        pltpu.VMEM((hp, tq, tq), BF16)],
        compiler_params=_params(("parallel", "parallel", "arbitrary")),
        name="mla_attention",
    )(qt, k, vt, z)


HALO = 32
SUBLANES = 8
PITCH = 164
PHASES = 8


def _conv_kernel(uval_ref, ugate_ref, cgate_ref, glub_ref, dww_ref, dwb_ref, lng_ref, lnb_ref, wpw_ref, bpw_ref,
                 o_ref, ubuf_ref, cbuf_ref, carry_ref):
    ts, c = uval_ref.shape
    seg = ts // SUBLANES
    n_slabs = c // LANES
    first = HALO - (CONV_K - 1)

    @pl.when(pl.program_id(1) == 0)
    def _():
        carry_ref[...] = jnp.zeros(carry_ref.shape, F32)

    a = uval_ref[...].astype(F32) + glub_ref[:, 0:c]
    g = ugate_ref[...].astype(F32) + glub_ref[:, c:2 * c]
    u = a * _sigmoid(g)
    for s in range(n_slabs):
        us = u[:, s * LANES:(s + 1) * LANES]
        ubuf_ref[s, first:HALO, :] = carry_ref[s, first:HALO, :]
        for k in range(SUBLANES):
            ubuf_ref[s, k * PITCH + HALO:k * PITCH + HALO + seg, :] = us[k * seg:(k + 1) * seg]
            if k + 1 < SUBLANES:
                ubuf_ref[s, (k + 1) * PITCH + first:(k + 1) * PITCH + HALO, :] = (
                    us[(k + 1) * seg - (HALO - first):(k + 1) * seg])
        carry_ref[s, first:HALO, :] = us[ts - (HALO - first):ts]

    def slab_body(s, carry):
        bias = dwb_ref[s]

        def group_body(gi, carry2):
            b0 = gi * PHASES
            accs = [None] * PHASES
            for r in range(CONV_K - 1 + PHASES):
                rows = ubuf_ref[s, pl.ds(b0 + first + r, SUBLANES, stride=PITCH), :]
                for ph in range(PHASES):
                    j = r - ph
                    if 0 <= j < CONV_K:
                        term = dww_ref[s, j:j + 1, :] * rows
                        accs[ph] = term if accs[ph] is None else accs[ph] + term
            for ph in range(PHASES):
                cbuf_ref[s, pl.ds(b0 + ph, SUBLANES, stride=seg), :] = accs[ph] + bias
            return carry2

        return lax.fori_loop(0, seg // PHASES, group_body, carry, unroll=True)

    lax.fori_loop(0, n_slabs, slab_body, 0)

    y = jnp.concatenate([cbuf_ref[s] for s in range(n_slabs)], axis=-1)
    mu = jnp.mean(y, axis=-1, keepdims=True)
    yc = y - mu
    var = jnp.mean(yc * yc, axis=-1, keepdims=True)
    yn = yc * lax.rsqrt(var + EPS) * lng_ref[...] + lnb_ref[...]
    act = _silu(yn).astype(BF16)
    pw = jnp.dot(act, wpw_ref[...], preferred_element_type=F32) + bpw_ref[...]
    o_ref[...] = (pw * _silu(cgate_ref[...].astype(F32))).astype(BF16)


def _conv_module(z, glub, dww, dwb, lng, lnb, wpw, bpw, batch, seq, ts):
    c = wpw.shape[0]
    n_slabs = c // LANES
    tiles_per_seq = seq // ts
    assert ts % (SUBLANES * PHASES) == 0 and HALO + ts // SUBLANES <= PITCH

    def zcol(offset):
        return pl.BlockSpec((ts, c), lambda b, i: (b * tiles_per_seq + i, offset // c))

    def full(arr):
        return pl.BlockSpec(arr.shape, lambda b, i: (0,) * arr.ndim)

    return pl.pallas_call(
        _conv_kernel,
        grid=(batch, tiles_per_seq),
        in_specs=[zcol(Z_CONV_IN), zcol(Z_CONV_IN + c), zcol(Z_CONV_GATE),
                  full(glub), full(dww), full(dwb), full(lng), full(lnb), full(wpw), full(bpw)],
        out_specs=pl.BlockSpec((ts, c), lambda b, i: (b * tiles_per_seq + i, 0)),
        out_shape=jax.ShapeDtypeStruct((batch * seq, c), BF16),
        scratch_shapes=[pltpu.VMEM((n_slabs, SUBLANES * PITCH, LANES), F32), pltpu.VMEM((n_slabs, ts, LANES), F32),
                        pltpu.VMEM((n_slabs, HALO, LANES), F32)],
        compiler_params=_params(("parallel", "arbitrary")),
        name="conv_module",
    )(z, z, z, glub, dww, dwb, lng, lnb, wpw, bpw)


def _slabs(w):
    rows, c = w.shape
    return w.reshape(rows, c // LANES, LANES).transpose(1, 0, 2)


def _outproj_kernel(mla_ref, conv_ref, x_ref, gate_ref, wa_ref, wb_ref, o_ref):
    y = jnp.dot(mla_ref[...], wa_ref[...], preferred_element_type=F32)
    y = y + jnp.dot(conv_ref[...], wb_ref[...], preferred_element_type=F32)
    o_ref[...] = x_ref[...] + gate_ref[0] * y


def _outproj(mla, conv, x2, mod3, w_a, w_b, seq, tm):
    t, d = x2.shape
    half = mla.shape[1]
    tiles_per_seq = seq // tm
    return pl.pallas_call(
        _outproj_kernel,
        grid=(t // tm,),
        in_specs=[
            pl.BlockSpec((tm, half), lambda i: (i, 0)),
            pl.BlockSpec((tm, half), lambda i: (i, 0)),
            pl.BlockSpec((tm, d), lambda i: (i, 0)),
            pl.BlockSpec((1, 1, d), lambda i: (i // tiles_per_seq, 0, 2)),
            pl.BlockSpec((half, d), lambda i: (0, 0), pipeline_mode=pl.Buffered(1)),
            pl.BlockSpec((half, d), lambda i: (0, 0), pipeline_mode=pl.Buffered(1)),
        ],
        out_specs=pl.BlockSpec((tm, d), lambda i: (i, 0)),
        out_shape=jax.ShapeDtypeStruct((t, d), F32),
        compiler_params=_params(("parallel",)),
        name="outproj",
    )(mla, conv, x2, mod3, w_a, w_b)


def _tile(n, want):
    t = min(n, want)
    assert n % t == 0, (n, t)
    return t


def kernel(x, c, positions, ada_w, ada_b, norm_g, w_in, q_lat_g, w_q_up, kv_lat_g, w_kv_up, q_norm_g, k_norm_g,
           glu_b, dw_w, dw_b, conv_ln_g, conv_ln_b, w_pw, b_pw, w_out):
    batch, seq, d = x.shape
    depth = ada_w.shape[0]
    t = batch * seq
    d_mla = N_HEADS * V_DIM
    d_conv = w_pw.shape[1]
    assert w_in.shape[2] == Q_LORA + KV_LORA + ROPE_DIM + d_mla + 3 * d_conv
    assert d_mla == 1024 and d_conv == 1024 and d == 2048

    inv_freq = 1.0 / (ROPE_THETA ** (jnp.arange(0, ROPE_DIM, 2, dtype=F32) / ROPE_DIM))
    pos_t = jnp.broadcast_to(positions.reshape(1, t), (ROPE_DIM // 2, t))
    inv_t = jnp.broadcast_to(inv_freq[:, None], (ROPE_DIM // 2, _tile(t, ROPE_TABLE_COLS)))
    cos_ft, sin_ft = _rope_tables(pos_t, inv_t)

    c_rows = 8
    c_pad = jnp.zeros((c_rows, d), F32).at[:batch].set(c)
    mod = _adaln(c_pad, ada_w, ada_b.reshape(depth, 1, 3 * d), _tile(3 * d, ADALN_COLS))

    x2 = x.reshape(t, d)
    s0 = Q_LORA
    s1 = s0 + KV_LORA
    s2 = s1 + ROPE_DIM
    s3 = s2 + d_mla
    s4 = s3 + 2 * d_conv
    w_z_t = _wprep(jnp.swapaxes(w_in, 1, 2), ((s3, s4), (s2, s3), (s4, w_in.shape[2]), (0, s0), (s0, s1), (s1, s2)),
                   WPREP_COLS)
    for l in range(depth):
        mod3 = mod[l].reshape(c_rows, 1, 3 * d)
        wqt = w_q_up[l].T.astype(BF16)
        wkv = w_kv_up[l].reshape(KV_LORA, N_HEADS, NOPE_DIM + V_DIM)
        wk = wkv[:, :, :NOPE_DIM].reshape(KV_LORA, N_HEADS * NOPE_DIM).astype(BF16)
        wvt = wkv[:, :, NOPE_DIM:].reshape(KV_LORA, N_HEADS * V_DIM).T.astype(BF16)
        gq, gk = q_norm_g[l], k_norm_g[l]
        gkn = gk[None, :NOPE_DIM]
        gkr = jnp.concatenate([gk[NOPE_DIM:], jnp.zeros((LANES - ROPE_DIM,), F32)])[None, :]

        z = _inproj(x2, mod3, norm_g[l][None, :], w_z_t, l, seq, _tile(seq, INPROJ_ROWS), Z_COLS)
        tm_qkv = _tile(seq, QKV_ROWS)
        qt, k, vt = _qkv_up(z, cos_ft, sin_ft, q_lat_g[l][None, :], kv_lat_g[l][None, :], wqt, wk, wvt,
                            jnp.broadcast_to(gq[:, None], (QK_DIM, tm_qkv)), gkn, gkr, batch, seq, tm_qkv)
        mla = _attention(qt, k, vt, z, _tile(seq, ATTN_QUERY_TILE), ATTN_HEADS_PER_STEP)
        dww = jnp.concatenate([dw_w[l], jnp.zeros((HALO - CONV_K, d_conv), F32)], axis=0)
        conv = _conv_module(z, glu_b[l][None, :], _slabs(dww), _slabs(dw_b[l][None, :]), conv_ln_g[l][None, :],
                            conv_ln_b[l][None, :], w_pw[l].astype(BF16), b_pw[l][None, :], batch, seq,
                            _tile(seq, CONV_ROWS))
        wo = w_out[l].astype(BF16)
        x2 = _outproj(mla, conv, x2, mod3, wo[:d_mla], wo[d_mla:], seq, _tile(seq, OUTPROJ_ROWS))
    return x2.reshape(batch, seq, d)
```

```python
import functools
import math

import jax
import jax.numpy as jnp
from jax import lax
from jax.experimental import pallas as pl
from jax.experimental.pallas import tpu as pltpu

F32 = jnp.float32
BF16 = jnp.bfloat16

N_HEADS = 8
NOPE_DIM = 128
ROPE_DIM = 64
V_DIM = 128
QK_DIM = NOPE_DIM + ROPE_DIM
Q_LORA = 512
KV_LORA = 256
ROPE_THETA = 10000.0
CONV_K = 31
EPS = 1e-6
LANES = 128
BF16_SUBLANES = 16
VT_ROWS = V_DIM + BF16_SUBLANES
_NT = (((1,), (1,)), ((), ()))

Z_CONV_IN = 0
Z_MLA_GATE = 2048
Z_CONV_GATE = 3072
Z_Q_LAT = 4096
Z_KV_LAT = 4608
Z_K_ROPE = 4864
Z_COLS = 5120

VMEM_LIMIT = 56 * 1024 * 1024

ROPE_TABLE_COLS = 2048
ADALN_COLS = 1024
WPREP_COLS = 256
INPROJ_ROWS = 512
QKV_ROWS = 512
ATTN_QUERY_TILE = 1024
ATTN_HEADS_PER_STEP = 4
ATTN_CHAIN_COLS = 256
CONV_ROWS = 1024
OUTPROJ_ROWS = 512


def _sigmoid(x):
    return 1.0 / (1.0 + jnp.exp2(x * (-math.log2(math.e))))


def _silu(x):
    return x * _sigmoid(x)


def _params(sem, vmem=VMEM_LIMIT):
    return pltpu.CompilerParams(dimension_semantics=sem, vmem_limit_bytes=vmem)


def _rope_table_kernel(pos_ref, inv_ref, cos_ref, sin_ref):
    ang = pos_ref[...].astype(F32) * inv_ref[...]
    cos_ref[...] = jnp.cos(ang)
    sin_ref[...] = jnp.sin(ang)


def _rope_tables(pos_t, inv_t):
    half, t = pos_t.shape
    tr = inv_t.shape[1]
    col = pl.BlockSpec((half, tr), lambda i: (0, i))
    return pl.pallas_call(
        _rope_table_kernel,
        grid=(t // tr,),
        in_specs=[col, pl.BlockSpec((half, tr), lambda i: (0, 0))],
        out_specs=[col, col],
        out_shape=[jax.ShapeDtypeStruct((half, t), F32)] * 2,
        compiler_params=_params(("parallel",)),
        name="rope_tables",
    )(pos_t, inv_t)


def _adaln_kernel(c_ref, w_ref, b_ref, o_ref):
    c_act = _silu(c_ref[...]).astype(BF16)
    acc = jnp.dot(c_act, w_ref[0].astype(BF16), preferred_element_type=F32)
    o_ref[0] = acc + b_ref[0]


def _adaln(c_pad, ada_w, ada_b3, tn):
    depth, d, n = ada_w.shape
    rows = c_pad.shape[0]
    return pl.pallas_call(
        _adaln_kernel,
        grid=(depth, n // tn),
        in_specs=[
            pl.BlockSpec((rows, d), lambda l, j: (0, 0)),
            pl.BlockSpec((1, d, tn), lambda l, j: (l, 0, j)),
            pl.BlockSpec((1, 1, tn), lambda l, j: (l, 0, j)),
        ],
        out_specs=pl.BlockSpec((1, rows, tn), lambda l, j: (l, 0, j)),
        out_shape=jax.ShapeDtypeStruct((depth, rows, n), F32),
        compiler_params=_params(("parallel", "parallel")),
        name="adaln_mod",
    )(c_pad, ada_w, ada_b3)


def _wprep_kernel(wt_ref, o_ref, *, segments):
    row = 0
    for a, b in segments:
        o_ref[0, row:row + (b - a), :] = wt_ref[0, a:b, :].astype(BF16)
        row += b - a
    o_ref[0, row:, :] = jnp.zeros((o_ref.shape[1] - row, o_ref.shape[2]), BF16)


def _wprep(w_in_t, segments, cb):
    depth, n, d = w_in_t.shape
    return pl.pallas_call(
        functools.partial(_wprep_kernel, segments=segments),
        grid=(depth, d // cb),
        in_specs=[pl.BlockSpec((1, n, cb), lambda l, i: (l, 0, i))],
        out_specs=pl.BlockSpec((1, Z_COLS, cb), lambda l, i: (l, 0, i)),
        out_shape=jax.ShapeDtypeStruct((depth, Z_COLS, d), BF16),
        compiler_params=_params(("parallel", "parallel")),
        name="w_in_layout",
    )(w_in_t)


def _inproj_kernel(x_ref, shift_ref, scale_ref, g_ref, w_ref, z_ref, h0_ref, h1_ref, *, nm, nn):
    i, j = pl.program_id(0), pl.program_id(1)
    tm, d = x_ref.shape
    rows_per_step = tm // nn
    h_refs = (h0_ref, h1_ref)

    def normalise(h_ref):
        rows = pl.ds(pl.multiple_of(j * rows_per_step, rows_per_step), rows_per_step)
        x = x_ref[rows, :]
        inv = lax.rsqrt(jnp.sum(x * x, axis=-1, keepdims=True) * (1.0 / d) + EPS)
        h_ref[rows, :] = (x * inv * (g_ref[...] * (1.0 + scale_ref[0])) + shift_ref[0]).astype(BF16)

    def project(h_ref):
        z_ref[...] = lax.dot_general(h_ref[...], w_ref[0], _NT, preferred_element_type=F32).astype(BF16)

    @pl.when(i == 0)
    def _():
        normalise(h_refs[0])

    for parity in range(2):
        @pl.when(jnp.logical_and(jnp.logical_and(i > 0, i < nm), i % 2 == parity))
        def _():
            project(h_refs[1 - parity])
            normalise(h_refs[parity])

    @pl.when(i == nm)
    def _():
        project(h_refs[(nm - 1) % 2])


def _inproj(x2, mod3, g, w_t, layer, seq, tm, tn):
    t, d = x2.shape
    ncol = w_t.shape[1]
    nm, nn = t // tm, ncol // tn
    tiles_per_seq = seq // tm
    assert tm % nn == 0

    def x_tile(i):
        return jnp.minimum(i, nm - 1)

    def mod_spec(chunk):
        return pl.BlockSpec((1, 1, d), lambda i, j: (x_tile(i) // tiles_per_seq, 0, chunk))

    def w_tile(i, j):
        return jnp.where(i == 0, 0, j)

    return pl.pallas_call(
        functools.partial(_inproj_kernel, nm=nm, nn=nn),
        grid=(nm + 1, nn),
        in_specs=[
            pl.BlockSpec((tm, d), lambda i, j: (x_tile(i), 0)),
            mod_spec(0),
            mod_spec(1),
            pl.BlockSpec((1, d), lambda i, j: (0, 0)),
            pl.BlockSpec((1, tn, d), lambda i, j: (layer, w_tile(i, j), 0),
                         pipeline_mode=pl.Buffered(1) if nn == 1 else None),
        ],
        out_specs=pl.BlockSpec((tm, tn), lambda i, j: (jnp.maximum(i - 1, 0), w_tile(i, j))),
        out_shape=jax.ShapeDtypeStruct((t, ncol), BF16),
        scratch_shapes=[pltpu.VMEM((tm, d), BF16), pltpu.VMEM((tm, d), BF16)],
        compiler_params=_params(("arbitrary", "arbitrary")),
        name="inproj",
    )(x2, mod3, mod3, g, w_t)


def _rms_rows(x, g):
    inv = lax.rsqrt(jnp.mean(x * x, axis=-1, keepdims=True) + EPS)
    return x * inv * g


def _rotate_half_pairs(x):
    lane = lax.broadcasted_iota(jnp.int32, x.shape, 1)
    first = (lane % ROPE_DIM) < (ROPE_DIM // 2)
    return jnp.where(first, pltpu.roll(x, LANES - ROPE_DIM // 2, 1), pltpu.roll(x, ROPE_DIM // 2, 1))


def _qkv_kernel(qlat_ref, kvlat_ref, krope_ref, cost_ref, sint_ref, gql_ref, gkvl_ref,
                wqt_ref, wk_ref, wvt_ref, gq_ref, gkn_ref, gkr_ref, qt_ref, k_ref, vt_ref):
    qn = _rms_rows(qlat_ref[...].astype(F32), gql_ref[...]).astype(BF16)
    kvn = _rms_rows(kvlat_ref[...].astype(F32), gkvl_ref[...]).astype(BF16)
    qt = lax.dot_general(wqt_ref[...], qn, _NT, preferred_element_type=F32)
    kn = jnp.dot(kvn, wk_ref[...], preferred_element_type=F32)
    vt = lax.dot_general(wvt_ref[...], kvn, _NT, preferred_element_type=F32)

    cost = cost_ref[...]
    sint = sint_ref[...]
    gq = gq_ref[...]
    half = ROPE_DIM // 2
    q_scale = math.log2(math.e) / math.sqrt(QK_DIM)
    for h in range(N_HEADS):
        blk = qt[h * QK_DIM:(h + 1) * QK_DIM, :]
        inv = lax.rsqrt(jnp.sum(blk * blk, axis=0, keepdims=True) * (1.0 / QK_DIM) + EPS) * q_scale
        y = blk * gq
        x1 = y[NOPE_DIM:NOPE_DIM + half, :]
        x2 = y[NOPE_DIM + half:QK_DIM, :]
        qt_ref[0, h, 0:NOPE_DIM, :] = (y[0:NOPE_DIM, :] * inv).astype(BF16)
        qt_ref[0, h, NOPE_DIM:NOPE_DIM + half, :] = ((x1 * cost - x2 * sint) * inv).astype(BF16)
        qt_ref[0, h, NOPE_DIM + half:QK_DIM, :] = ((x2 * cost + x1 * sint) * inv).astype(BF16)

    kr = krope_ref[...].astype(F32)
    kr_ss = jnp.sum(kr * kr, axis=-1, keepdims=True)
    krg = kr * gkr_ref[...]
    cos_tok = jnp.concatenate([cost] * (LANES // half), axis=0).T
    sin_tok = jnp.concatenate([-sint, sint] * (LANES // ROPE_DIM), axis=0).T
    kr_rot = krg * cos_tok + _rotate_half_pairs(krg) * sin_tok
    ones_rows = jnp.ones((VT_ROWS - V_DIM, vt.shape[1]), BF16)
    for h in range(N_HEADS):
        k_nope = kn[:, h * NOPE_DIM:(h + 1) * NOPE_DIM]
        ss = jnp.sum(k_nope * k_nope, axis=-1, keepdims=True) + kr_ss
        inv = lax.rsqrt(ss * (1.0 / QK_DIM) + EPS)
        k_ref[0, h, :, 0:NOPE_DIM] = (k_nope * inv * gkn_ref[...]).astype(BF16)
        k_ref[0, h, :, NOPE_DIM:QK_DIM] = (kr_rot[:, 0:ROPE_DIM] * inv).astype(BF16)
        vt_ref[0, h, 0:V_DIM, :] = vt[h * V_DIM:(h + 1) * V_DIM, :].astype(BF16)
        vt_ref[0, h, V_DIM:VT_ROWS, :] = ones_rows


def _qkv_up(z, cos_ft, sin_ft, gql, gkvl, wqt, wk, wvt, gq, gkn, gkr, batch, seq, tm):
    tiles_per_seq = seq // tm

    def zcol(width, offset):
        return pl.BlockSpec((tm, width), lambda i: (i, offset // width))

    def full(arr):
        return pl.BlockSpec(arr.shape, lambda i: (0,) * arr.ndim)

    def feat_major(rows):
        return pl.BlockSpec((1, N_HEADS, rows, tm), lambda i: (i // tiles_per_seq, 0, 0, i % tiles_per_seq))

    k_out = pl.BlockSpec((1, N_HEADS, tm, QK_DIM), lambda i: (i // tiles_per_seq, 0, i % tiles_per_seq, 0))
    col = pl.BlockSpec((ROPE_DIM // 2, tm), lambda i: (0, i))
    return pl.pallas_call(
        _qkv_kernel,
        grid=(batch * tiles_per_seq,),
        in_specs=[zcol(Q_LORA, Z_Q_LAT), zcol(KV_LORA, Z_KV_LAT), zcol(LANES, Z_K_ROPE), col, col,
                  full(gql), full(gkvl), full(wqt), full(wk), full(wvt), full(gq), full(gkn), full(gkr)],
        out_specs=[feat_major(QK_DIM), k_out, feat_major(VT_ROWS)],
        out_shape=[jax.ShapeDtypeStruct((batch, N_HEADS, QK_DIM, seq), BF16),
                   jax.ShapeDtypeStruct((batch, N_HEADS, seq, QK_DIM), BF16),
                   jax.ShapeDtypeStruct((batch, N_HEADS, VT_ROWS, seq), BF16)],
        compiler_params=_params(("parallel",)),
        name="qkv_up",
    )(z, z, z, cos_ft, sin_ft, gql, gkvl, wqt, wk, wvt, gq, gkn, gkr)


def _attn_kernel(qt_ref, k_ref, vt_ref, gate_ref, o_ref, m_ref, alpha_ref, acc_ref, p_ref, *, tq, qc, hp):
    tk = tq
    qi = pl.program_id(2)
    m_ref[...] = jnp.full(m_ref.shape, -jnp.inf, F32)
    acc_ref[...] = jnp.zeros(acc_ref.shape, F32)
    chains = tuple((h, c) for h in range(hp) for c in range(tq // qc))

    def qcols(c):
        return slice(c * qc, (c + 1) * qc)

    def keys_of(kb):
        return pl.ds(pl.multiple_of(kb * tk, tk), tk)

    def qk(k, h, c):
        return jnp.dot(k, qt_ref[0, h, :, qcols(c)], preferred_element_type=F32)

    def softmax(h, c, parts):
        m_prev = m_ref[h, :, qcols(c)]
        m_new = m_prev
        for _, st in parts:
            m_new = jnp.maximum(m_new, jnp.max(st, axis=0, keepdims=True))
        alpha_ref[h, :, qcols(c)] = jnp.exp2(m_prev - m_new)
        for r0, st in parts:
            p_ref[h, r0:r0 + st.shape[0], qcols(c)] = jnp.exp2(st - m_new).astype(BF16)
        m_ref[h, :, qcols(c)] = m_new

    def pv(kb):
        vts = [vt_ref[0, h, :, keys_of(kb)] for h in range(hp)]
        for h, c in chains:
            acc_ref[h, :, qcols(c)] = (alpha_ref[h, :, qcols(c)] * acc_ref[h, :, qcols(c)]
                                       + jnp.dot(vts[h], p_ref[h, :, qcols(c)], preferred_element_type=F32))

    n_full = qi
    d0 = pl.multiple_of(qi * tq, tq)
    sts = [qk(k_ref[0, h, pl.ds(d0, (c + 1) * qc), :], h, c) for h, c in chains]
    tri = (lax.broadcasted_iota(jnp.int32, (qc, qc), 0) <= lax.broadcasted_iota(jnp.int32, (qc, qc), 1))
    for (h, c), st in zip(chains, sts):
        parts = [(c * qc, jnp.where(tri, st[c * qc:, :], -jnp.inf))]
        if c:
            parts.insert(0, (0, st[:c * qc, :]))
        softmax(h, c, parts)
        if (c + 1) * qc < tk:
            p_ref[h, (c + 1) * qc:tk, qcols(c)] = jnp.zeros((tk - (c + 1) * qc, qc), BF16)

    def body(kb, carry):
        ks = [k_ref[0, h, keys_of(kb), :] for h in range(hp)]
        sts = [qk(ks[h], h, c) for h, c in chains]
        pv(jnp.where(kb == 0, n_full, kb - 1))
        for (h, c), st in zip(chains, sts):
            softmax(h, c, [(0, st)])
        return carry

    lax.fori_loop(0, n_full, body, 0)
    pv(jnp.where(n_full == 0, n_full, n_full - 1))

    for h in range(hp):
        acc = acc_ref[h]
        out_t = acc[0:V_DIM, :] / acc[V_DIM:V_DIM + 1, :]
        gate = gate_ref[:, h * V_DIM:(h + 1) * V_DIM].astype(F32)
        o_ref[:, h * V_DIM:(h + 1) * V_DIM] = (out_t.T * _silu(gate)).astype(BF16)


def _attention(qt, k, vt, z, tq, hp):
    batch, heads, _, seq = qt.shape
    nq = seq // tq
    gate_col0 = Z_MLA_GATE // (hp * V_DIM)
    return pl.pallas_call(
        functools.partial(_attn_kernel, tq=tq, qc=min(tq, ATTN_CHAIN_COLS), hp=hp),
        grid=(batch, heads // hp, nq),
        in_specs=[
            pl.BlockSpec((1, hp, QK_DIM, tq), lambda b, h, i: (b, h, 0, i)),
            pl.BlockSpec((1, hp, seq, QK_DIM), lambda b, h, i: (b, h, 0, 0)),
            pl.BlockSpec((1, hp, VT_ROWS, seq), lambda b, h, i: (b, h, 0, 0)),
            pl.BlockSpec((tq, hp * V_DIM), lambda b, h, i: (b * nq + i, gate_col0 + h)),
        ],
        out_specs=pl.BlockSpec((tq, hp * V_DIM), lambda b, h, i: (b * nq + i, h)),
        out_shape=jax.ShapeDtypeStruct((batch * seq, heads * V_DIM), BF16),
        scratch_shapes=[pltpu.VMEM((hp, 1, tq), F32), pltpu.VMEM((hp, 1, tq), F32), pltpu.VMEM((hp, VT_ROWS, tq), F32),
                        pltpu.VMEM((hp, tq, tq), BF16)],
        compiler_params=_params(("parallel", "parallel", "arbitrary")),
        name="mla_attention",
    )(qt, k, vt, z)


HALO = 32
SUBLANES = 8
PITCH = 164
PHASES = 8


def _conv_kernel(uval_ref, ugate_ref, cgate_ref, glub_ref, dww_ref, dwb_ref, lng_ref, lnb_ref, wpw_ref, bpw_ref,
                 o_ref, ubuf_ref, cbuf_ref, carry_ref):
    ts, c = uval_ref.shape
    seg = ts // SUBLANES
    n_slabs = c // LANES
    first = HALO - (CONV_K - 1)

    @pl.when(pl.program_id(1) == 0)
    def _():
        carry_ref[...] = jnp.zeros(carry_ref.shape, F32)

    a = uval_ref[...].astype(F32) + glub_ref[:, 0:c]
    g = ugate_ref[...].astype(F32) + glub_ref[:, c:2 * c]
    u = a * _sigmoid(g)
    for s in range(n_slabs):
        us = u[:, s * LANES:(s + 1) * LANES]
        ubuf_ref[s, first:HALO, :] = carry_ref[s, first:HALO, :]
        for k in range(SUBLANES):
            ubuf_ref[s, k * PITCH + HALO:k * PITCH + HALO + seg, :] = us[k * seg:(k + 1) * seg]
            if k + 1 < SUBLANES:
                ubuf_ref[s, (k + 1) * PITCH + first:(k + 1) * PITCH + HALO, :] = (
                    us[(k + 1) * seg - (HALO - first):(k + 1) * seg])
        carry_ref[s, first:HALO, :] = us[ts - (HALO - first):ts]

    def slab_body(s, carry):
        bias = dwb_ref[s]

        def group_body(gi, carry2):
            b0 = gi * PHASES
            accs = [None] * PHASES
            for r in range(CONV_K - 1 + PHASES):
                rows = ubuf_ref[s, pl.ds(b0 + first + r, SUBLANES, stride=PITCH), :]
                for ph in range(PHASES):
                    j = r - ph
                    if 0 <= j < CONV_K:
                        term = dww_ref[s, j:j + 1, :] * rows
                        accs[ph] = term if accs[ph] is None else accs[ph] + term
            for ph in range(PHASES):
                cbuf_ref[s, pl.ds(b0 + ph, SUBLANES, stride=seg), :] = accs[ph] + bias
            return carry2

        return lax.fori_loop(0, seg // PHASES, group_body, carry, unroll=True)

    lax.fori_loop(0, n_slabs, slab_body, 0)

    y = jnp.concatenate([cbuf_ref[s] for s in range(n_slabs)], axis=-1)
    mu = jnp.mean(y, axis=-1, keepdims=True)
    yc = y - mu
    var = jnp.mean(yc * yc, axis=-1, keepdims=True)
    yn = yc * lax.rsqrt(var + EPS) * lng_ref[...] + lnb_ref[...]
    act = _silu(yn).astype(BF16)
    pw = jnp.dot(act, wpw_ref[...], preferred_element_type=F32) + bpw_ref[...]
    o_ref[...] = (pw * _silu(cgate_ref[...].astype(F32))).astype(BF16)


def _conv_module(z, glub, dww, dwb, lng, lnb, wpw, bpw, batch, seq, ts):
    c = wpw.shape[0]
    n_slabs = c // LANES
    tiles_per_seq = seq // ts
    assert ts % (SUBLANES * PHASES) == 0 and HALO + ts // SUBLANES <= PITCH

    def zcol(offset):
        return pl.BlockSpec((ts, c), lambda b, i: (b * tiles_per_seq + i, offset // c))

    def full(arr):
        return pl.BlockSpec(arr.shape, lambda b, i: (0,) * arr.ndim)

    return pl.pallas_call(
        _conv_kernel,
        grid=(batch, tiles_per_seq),
        in_specs=[zcol(Z_CONV_IN), zcol(Z_CONV_IN + c), zcol(Z_CONV_GATE),
                  full(glub), full(dww), full(dwb), full(lng), full(lnb), full(wpw), full(bpw)],
        out_specs=pl.BlockSpec((ts, c), lambda b, i: (b * tiles_per_seq + i, 0)),
        out_shape=jax.ShapeDtypeStruct((batch * seq, c), BF16),
        scratch_shapes=[pltpu.VMEM((n_slabs, SUBLANES * PITCH, LANES), F32), pltpu.VMEM((n_slabs, ts, LANES), F32),
                        pltpu.VMEM((n_slabs, HALO, LANES), F32)],
        compiler_params=_params(("parallel", "arbitrary")),
        name="conv_module",
    )(z, z, z, glub, dww, dwb, lng, lnb, wpw, bpw)


def _slabs(w):
    rows, c = w.shape
    return w.reshape(rows, c // LANES, LANES).transpose(1, 0, 2)


def _outproj_kernel(mla_ref, conv_ref, x_ref, gate_ref, wa_ref, wb_ref, o_ref):
    y = jnp.dot(mla_ref[...], wa_ref[...], preferred_element_type=F32)
    y = y + jnp.dot(conv_ref[...], wb_ref[...], preferred_element_type=F32)
    o_ref[...] = x_ref[...] + gate_ref[0] * y


def _outproj(mla, conv, x2, mod3, w_a, w_b, seq, tm):
    t, d = x2.shape
    half = mla.shape[1]
    tiles_per_seq = seq // tm
    return pl.pallas_call(
        _outproj_kernel,
        grid=(t // tm,),
        in_specs=[
            pl.BlockSpec((tm, half), lambda i: (i, 0)),
            pl.BlockSpec((tm, half), lambda i: (i, 0)),
            pl.BlockSpec((tm, d), lambda i: (i, 0)),
            pl.BlockSpec((1, 1, d), lambda i: (i // tiles_per_seq, 0, 2)),
            pl.BlockSpec((half, d), lambda i: (0, 0), pipeline_mode=pl.Buffered(1)),
            pl.BlockSpec((half, d), lambda i: (0, 0), pipeline_mode=pl.Buffered(1)),
        ],
        out_specs=pl.BlockSpec((tm, d), lambda i: (i, 0)),
        out_shape=jax.ShapeDtypeStruct((t, d), F32),
        compiler_params=_params(("parallel",)),
        name="outproj",
    )(mla, conv, x2, mod3, w_a, w_b)


def _tile(n, want):
    t = min(n, want)
    assert n % t == 0, (n, t)
    return t


def kernel(x, c, positions, ada_w, ada_b, norm_g, w_in, q_lat_g, w_q_up, kv_lat_g, w_kv_up, q_norm_g, k_norm_g,
           glu_b, dw_w, dw_b, conv_ln_g, conv_ln_b, w_pw, b_pw, w_out):
    batch, seq, d = x.shape
    depth = ada_w.shape[0]
    t = batch * seq
    d_mla = N_HEADS * V_DIM
    d_conv = w_pw.shape[1]
    assert w_in.shape[2] == Q_LORA + KV_LORA + ROPE_DIM + d_mla + 3 * d_conv
    assert d_mla == 1024 and d_conv == 1024 and d == 2048

    inv_freq = 1.0 / (ROPE_THETA ** (jnp.arange(0, ROPE_DIM, 2, dtype=F32) / ROPE_DIM))
    pos_t = jnp.broadcast_to(positions.reshape(1, t), (ROPE_DIM // 2, t))
    inv_t = jnp.broadcast_to(inv_freq[:, None], (ROPE_DIM // 2, _tile(t, ROPE_TABLE_COLS)))
    cos_ft, sin_ft = _rope_tables(pos_t, inv_t)

    c_rows = 8
    c_pad = jnp.zeros((c_rows, d), F32).at[:batch].set(c)
    mod = _adaln(c_pad, ada_w, ada_b.reshape(depth, 1, 3 * d), _tile(3 * d, ADALN_COLS))

    x2 = x.reshape(t, d)
    s0 = Q_LORA
    s1 = s0 + KV_LORA
    s2 = s1 + ROPE_DIM
    s3 = s2 + d_mla
    s4 = s3 + 2 * d_conv
    w_z_t = _wprep(jnp.swapaxes(w_in, 1, 2), ((s3, s4), (s2, s3), (s4, w_in.shape[2]), (0, s0), (s0, s1), (s1, s2)),
                   WPREP_COLS)
    for l in range(depth):
        mod3 = mod[l].reshape(c_rows, 1, 3 * d)
        wqt = w_q_up[l].T.astype(BF16)
        wkv = w_kv_up[l].reshape(KV_LORA, N_HEADS, NOPE_DIM + V_DIM)
        wk = wkv[:, :, :NOPE_DIM].reshape(KV_LORA, N_HEADS * NOPE_DIM).astype(BF16)
        wvt = wkv[:, :, NOPE_DIM:].reshape(KV_LORA, N_HEADS * V_DIM).T.astype(BF16)
        gq, gk = q_norm_g[l], k_norm_g[l]
        gkn = gk[None, :NOPE_DIM]
        gkr = jnp.concatenate([gk[NOPE_DIM:], jnp.zeros((LANES - ROPE_DIM,), F32)])[None, :]

        z = _inproj(x2, mod3, norm_g[l][None, :], w_z_t, l, seq, _tile(seq, INPROJ_ROWS), Z_COLS)
        tm_qkv = _tile(seq, QKV_ROWS)
        qt, k, vt = _qkv_up(z, cos_ft, sin_ft, q_lat_g[l][None, :], kv_lat_g[l][None, :], wqt, wk, wvt,
                            jnp.broadcast_to(gq[:, None], (QK_DIM, tm_qkv)), gkn, gkr, batch, seq, tm_qkv)
        mla = _attention(qt, k, vt, z, _tile(seq, ATTN_QUERY_TILE), ATTN_HEADS_PER_STEP)
        dww = jnp.concatenate([dw_w[l], jnp.zeros((HALO - CONV_K, d_conv), F32)], axis=0)
        conv = _conv_module(z, glu_b[l][None, :], _slabs(dww), _slabs(dw_b[l][None, :]), conv_ln_g[l][None, :],
                            conv_ln_b[l][None, :], w_pw[l].astype(BF16), b_pw[l][None, :], batch, seq,
                            _tile(seq, CONV_ROWS))
        wo = w_out[l].astype(BF16)
        x2 = _outproj(mla, conv, x2, mod3, wo[:d_mla], wo[d_mla:], seq, _tile(seq, OUTPROJ_ROWS))
    return x2.reshape(batch, seq, d)
```

```python
import functools
import math

import jax
import jax.numpy as jnp
from jax import lax
from jax.experimental import pallas as pl
from jax.experimental.pallas import tpu as pltpu

F32 = jnp.float32
BF16 = jnp.bfloat16

N_HEADS = 8
NOPE_DIM = 128
ROPE_DIM = 64
V_DIM = 128
QK_DIM = NOPE_DIM + ROPE_DIM
Q_LORA = 512
KV_LORA = 256
ROPE_THETA = 10000.0
CONV_K = 31
EPS = 1e-6
LANES = 128
BF16_SUBLANES = 16
VT_ROWS = V_DIM + BF16_SUBLANES
_NT = (((1,), (1,)), ((), ()))

Z_CONV_IN = 0
Z_MLA_GATE = 2048
Z_CONV_GATE = 3072
Z_Q_LAT = 4096
Z_KV_LAT = 4608
Z_K_ROPE = 4864
Z_COLS = 5120

VMEM_LIMIT = 56 * 1024 * 1024

ROPE_TABLE_COLS = 2048
ADALN_COLS = 1024
WPREP_COLS = 256
INPROJ_ROWS = 512
QKV_ROWS = 512
ATTN_QUERY_TILE = 1024
ATTN_HEADS_PER_STEP = 2
ATTN_CHAIN_COLS = 256
CONV_ROWS = 1024
OUTPROJ_ROWS = 512


def _sigmoid(x):
    return 1.0 / (1.0 + jnp.exp2(x * (-math.log2(math.e))))


def _silu(x):
    return x * _sigmoid(x)


def _params(sem, vmem=VMEM_LIMIT):
    return pltpu.CompilerParams(dimension_semantics=sem, vmem_limit_bytes=vmem)


def _rope_table_kernel(pos_ref, inv_ref, cos_ref, sin_ref):
    ang = pos_ref[...].astype(F32) * inv_ref[...]
    cos_ref[...] = jnp.cos(ang)
    sin_ref[...] = jnp.sin(ang)


def _rope_tables(pos_t, inv_t):
    half, t = pos_t.shape
    tr = inv_t.shape[1]
    col = pl.BlockSpec((half, tr), lambda i: (0, i))
    return pl.pallas_call(
        _rope_table_kernel,
        grid=(t // tr,),
        in_specs=[col, pl.BlockSpec((half, tr), lambda i: (0, 0))],
        out_specs=[col, col],
        out_shape=[jax.ShapeDtypeStruct((half, t), F32)] * 2,
        compiler_params=_params(("parallel",)),
        name="rope_tables",
    )(pos_t, inv_t)


def _adaln_kernel(c_ref, w_ref, b_ref, o_ref):
    c_act = _silu(c_ref[...]).astype(BF16)
    acc = jnp.dot(c_act, w_ref[0].astype(BF16), preferred_element_type=F32)
    o_ref[0] = acc + b_ref[0]


def _adaln(c_pad, ada_w, ada_b3, tn):
    depth, d, n = ada_w.shape
    rows = c_pad.shape[0]
    return pl.pallas_call(
        _adaln_kernel,
        grid=(depth, n // tn),
        in_specs=[
            pl.BlockSpec((rows, d), lambda l, j: (0, 0)),
            pl.BlockSpec((1, d, tn), lambda l, j: (l, 0, j)),
            pl.BlockSpec((1, 1, tn), lambda l, j: (l, 0, j)),
        ],
        out_specs=pl.BlockSpec((1, rows, tn), lambda l, j: (l, 0, j)),
        out_shape=jax.ShapeDtypeStruct((depth, rows, n), F32),
        compiler_params=_params(("parallel", "parallel")),
        name="adaln_mod",
    )(c_pad, ada_w, ada_b3)


def _wprep_kernel(wt_ref, o_ref, *, segments):
    row = 0
    for a, b in segments:
        o_ref[0, row:row + (b - a), :] = wt_ref[0, a:b, :].astype(BF16)
        row += b - a
    o_ref[0, row:, :] = jnp.zeros((o_ref.shape[1] - row, o_ref.shape[2]), BF16)


def _wprep(w_in_t, segments, cb):
    depth, n, d = w_in_t.shape
    return pl.pallas_call(
        functools.partial(_wprep_kernel, segments=segments),
        grid=(depth, d // cb),
        in_specs=[pl.BlockSpec((1, n, cb), lambda l, i: (l, 0, i))],
        out_specs=pl.BlockSpec((1, Z_COLS, cb), lambda l, i: (l, 0, i)),
        out_shape=jax.ShapeDtypeStruct((depth, Z_COLS, d), BF16),
        compiler_params=_params(("parallel", "parallel")),
        name="w_in_layout",
    )(w_in_t)


def _inproj_kernel(x_ref, shift_ref, scale_ref, g_ref, w_ref, z_ref, h0_ref, h1_ref, *, nm, nn):
    i, j = pl.program_id(0), pl.program_id(1)
    tm, d = x_ref.shape
    rows_per_step = tm // nn
    h_refs = (h0_ref, h1_ref)

    def normalise(h_ref):
        rows = pl.ds(pl.multiple_of(j * rows_per_step, rows_per_step), rows_per_step)
        x = x_ref[rows, :]
        inv = lax.rsqrt(jnp.sum(x * x, axis=-1, keepdims=True) * (1.0 / d) + EPS)
        h_ref[rows, :] = (x * inv * (g_ref[...] * (1.0 + scale_ref[0])) + shift_ref[0]).astype(BF16)

    def project(h_ref):
        z_ref[...] = lax.dot_general(h_ref[...], w_ref[0], _NT, preferred_element_type=F32).astype(BF16)

    @pl.when(i == 0)
    def _():
        normalise(h_refs[0])

    for parity in range(2):
        @pl.when(jnp.logical_and(jnp.logical_and(i > 0, i < nm), i % 2 == parity))
        def _():
            project(h_refs[1 - parity])
            normalise(h_refs[parity])

    @pl.when(i == nm)
    def _():
        project(h_refs[(nm - 1) % 2])


def _inproj(x2, mod3, g, w_t, layer, seq, tm, tn):
    t, d = x2.shape
    ncol = w_t.shape[1]
    nm, nn = t // tm, ncol // tn
    tiles_per_seq = seq // tm
    assert tm % nn == 0

    def x_tile(i):
        return jnp.minimum(i, nm - 1)

    def mod_spec(chunk):
        return pl.BlockSpec((1, 1, d), lambda i, j: (x_tile(i) // tiles_per_seq, 0, chunk))

    def w_tile(i, j):
        return jnp.where(i == 0, 0, j)

    return pl.pallas_call(
        functools.partial(_inproj_kernel, nm=nm, nn=nn),
        grid=(nm + 1, nn),
        in_specs=[
            pl.BlockSpec((tm, d), lambda i, j: (x_tile(i), 0)),
            mod_spec(0),
            mod_spec(1),
            pl.BlockSpec((1, d), lambda i, j: (0, 0)),
            pl.BlockSpec((1, tn, d), lambda i, j: (layer, w_tile(i, j), 0),
                         pipeline_mode=pl.Buffered(1) if nn == 1 else None),
        ],
        out_specs=pl.BlockSpec((tm, tn), lambda i, j: (jnp.maximum(i - 1, 0), w_tile(i, j))),
        out_shape=jax.ShapeDtypeStruct((t, ncol), BF16),
        scratch_shapes=[pltpu.VMEM((tm, d), BF16), pltpu.VMEM((tm, d), BF16)],
        compiler_params=_params(("arbitrary", "arbitrary")),
        name="inproj",
    )(x2, mod3, mod3, g, w_t)


def _rms_rows(x, g):
    inv = lax.rsqrt(jnp.mean(x * x, axis=-1, keepdims=True) + EPS)
    return x * inv * g


def _rotate_half_pairs(x):
    lane = lax.broadcasted_iota(jnp.int32, x.shape, 1)
    first = (lane % ROPE_DIM) < (ROPE_DIM // 2)
    return jnp.where(first, pltpu.roll(x, LANES - ROPE_DIM // 2, 1), pltpu.roll(x, ROPE_DIM // 2, 1))


def _qkv_kernel(qlat_ref, kvlat_ref, krope_ref, cost_ref, sint_ref, gql_ref, gkvl_ref,
                wqt_ref, wk_ref, wvt_ref, gq_ref, gkn_ref, gkr_ref, qt_ref, k_ref, vt_ref):
    qn = _rms_rows(qlat_ref[...].astype(F32), gql_ref[...]).astype(BF16)
    kvn = _rms_rows(kvlat_ref[...].astype(F32), gkvl_ref[...]).astype(BF16)
    qt = lax.dot_general(wqt_ref[...], qn, _NT, preferred_element_type=F32)
    kn = jnp.dot(kvn, wk_ref[...], preferred_element_type=F32)
    vt = lax.dot_general(wvt_ref[...], kvn, _NT, preferred_element_type=F32)

    cost = cost_ref[...]
    sint = sint_ref[...]
    gq = gq_ref[...]
    half = ROPE_DIM // 2
    q_scale = math.log2(math.e) / math.sqrt(QK_DIM)
    for h in range(N_HEADS):
        blk = qt[h * QK_DIM:(h + 1) * QK_DIM, :]
        inv = lax.rsqrt(jnp.sum(blk * blk, axis=0, keepdims=True) * (1.0 / QK_DIM) + EPS) * q_scale
        y = blk * gq
        x1 = y[NOPE_DIM:NOPE_DIM + half, :]
        x2 = y[NOPE_DIM + half:QK_DIM, :]
        qt_ref[0, h, 0:NOPE_DIM, :] = (y[0:NOPE_DIM, :] * inv).astype(BF16)
        qt_ref[0, h, NOPE_DIM:NOPE_DIM + half, :] = ((x1 * cost - x2 * sint) * inv).astype(BF16)
        qt_ref[0, h, NOPE_DIM + half:QK_DIM, :] = ((x2 * cost + x1 * sint) * inv).astype(BF16)

    kr = krope_ref[...].astype(F32)
    kr_ss = jnp.sum(kr * kr, axis=-1, keepdims=True)
    krg = kr * gkr_ref[...]
    cos_tok = jnp.concatenate([cost] * (LANES // half), axis=0).T
    sin_tok = jnp.concatenate([-sint, sint] * (LANES // ROPE_DIM), axis=0).T
    kr_rot = krg * cos_tok + _rotate_half_pairs(krg) * sin_tok
    ones_rows = jnp.ones((VT_ROWS - V_DIM, vt.shape[1]), BF16)
    for h in range(N_HEADS):
        k_nope = kn[:, h * NOPE_DIM:(h + 1) * NOPE_DIM]
        ss = jnp.sum(k_nope * k_nope, axis=-1, keepdims=True) + kr_ss
        inv = lax.rsqrt(ss * (1.0 / QK_DIM) + EPS)
        k_ref[0, h, :, 0:NOPE_DIM] = (k_nope * inv * gkn_ref[...]).astype(BF16)
        k_ref[0, h, :, NOPE_DIM:QK_DIM] = (kr_rot[:, 0:ROPE_DIM] * inv).astype(BF16)
        vt_ref[0, h, 0:V_DIM, :] = vt[h * V_DIM:(h + 1) * V_DIM, :].astype(BF16)
        vt_ref[0, h, V_DIM:VT_ROWS, :] = ones_rows


def _qkv_up(z, cos_ft, sin_ft, gql, gkvl, wqt, wk, wvt, gq, gkn, gkr, batch, seq, tm):
    tiles_per_seq = seq // tm

    def zcol(width, offset):
        return pl.BlockSpec((tm, width), lambda i: (i, offset // width))

    def full(arr):
        return pl.BlockSpec(arr.shape, lambda i: (0,) * arr.ndim)

    def feat_major(rows):
        return pl.BlockSpec((1, N_HEADS, rows, tm), lambda i: (i // tiles_per_seq, 0, 0, i % tiles_per_seq))

    k_out = pl.BlockSpec((1, N_HEADS, tm, QK_DIM), lambda i: (i // tiles_per_seq, 0, i % tiles_per_seq, 0))
    col = pl.BlockSpec((ROPE_DIM // 2, tm), lambda i: (0, i))
    return pl.pallas_call(
        _qkv_kernel,
        grid=(batch * tiles_per_seq,),
        in_specs=[zcol(Q_LORA, Z_Q_LAT), zcol(KV_LORA, Z_KV_LAT), zcol(LANES, Z_K_ROPE), col, col,
                  full(gql), full(gkvl), full(wqt), full(wk), full(wvt), full(gq), full(gkn), full(gkr)],
        out_specs=[feat_major(QK_DIM), k_out, feat_major(VT_ROWS)],
        out_shape=[jax.ShapeDtypeStruct((batch, N_HEADS, QK_DIM, seq), BF16),
                   jax.ShapeDtypeStruct((batch, N_HEADS, seq, QK_DIM), BF16),
                   jax.ShapeDtypeStruct((batch, N_HEADS, VT_ROWS, seq), BF16)],
        compiler_params=_params(("parallel",)),
        name="qkv_up",
    )(z, z, z, cos_ft, sin_ft, gql, gkvl, wqt, wk, wvt, gq, gkn, gkr)


def _attn_kernel(qt_ref, k_ref, vt_ref, gate_ref, o_ref, m_ref, alpha_ref, acc_ref, p_ref, *, tq, qc, hp):
    tk = tq
    qi = pl.program_id(2)
    m_ref[...] = jnp.full(m_ref.shape, -jnp.inf, F32)
    acc_ref[...] = jnp.zeros(acc_ref.shape, F32)
    chains = tuple((h, c) for h in range(hp) for c in range(tq // qc))

    def qcols(c):
        return slice(c * qc, (c + 1) * qc)

    def keys_of(kb):
        return pl.ds(pl.multiple_of(kb * tk, tk), tk)

    def qk(k, h, c):
        return jnp.dot(k, qt_ref[0, h, :, qcols(c)], preferred_element_type=F32)

    def softmax(h, c, parts):
        m_prev = m_ref[h, :, qcols(c)]
        m_new = m_prev
        for _, st in parts:
            m_new = jnp.maximum(m_new, jnp.max(st, axis=0, keepdims=True))
        alpha_ref[h, :, qcols(c)] = jnp.exp2(m_prev - m_new)
        for r0, st in parts:
            p_ref[h, r0:r0 + st.shape[0], qcols(c)] = jnp.exp2(st - m_new).astype(BF16)
        m_ref[h, :, qcols(c)] = m_new

    def pv(kb):
        vts = [vt_ref[0, h, :, keys_of(kb)] for h in range(hp)]
        for h, c in chains:
            acc_ref[h, :, qcols(c)] = (alpha_ref[h, :, qcols(c)] * acc_ref[h, :, qcols(c)]
                                       + jnp.dot(vts[h], p_ref[h, :, qcols(c)], preferred_element_type=F32))

    n_full = qi
    d0 = pl.multiple_of(qi * tq, tq)
    sts = [qk(k_ref[0, h, pl.ds(d0, (c + 1) * qc), :], h, c) for h, c in chains]
    tri = (lax.broadcasted_iota(jnp.int32, (qc, qc), 0) <= lax.broadcasted_iota(jnp.int32, (qc, qc), 1))
    for (h, c), st in zip(chains, sts):
        parts = [(c * qc, jnp.where(tri, st[c * qc:, :], -jnp.inf))]
        if c:
            parts.insert(0, (0, st[:c * qc, :]))
        softmax(h, c, parts)
        if (c + 1) * qc < tk:
            p_ref[h, (c + 1) * qc:tk, qcols(c)] = jnp.zeros((tk - (c + 1) * qc, qc), BF16)

    def body(kb, carry):
        ks = [k_ref[0, h, keys_of(kb), :] for h in range(hp)]
        sts = [qk(ks[h], h, c) for h, c in chains]
        pv(jnp.where(kb == 0, n_full, kb - 1))
        for (h, c), st in zip(chains, sts):
            softmax(h, c, [(0, st)])
        return carry

    lax.fori_loop(0, n_full, body, 0)
    pv(jnp.where(n_full == 0, n_full, n_full - 1))

    for h in range(hp):
        acc = acc_ref[h]
        out_t = acc[0:V_DIM, :] / acc[V_DIM:V_DIM + 1, :]
        gate = gate_ref[:, h * V_DIM:(h + 1) * V_DIM].astype(F32)
        o_ref[:, h * V_DIM:(h + 1) * V_DIM] = (out_t.T * _silu(gate)).astype(BF16)


def _attention(qt, k, vt, z, tq, hp):
    batch, heads, _, seq = qt.shape
    nq = seq // tq
    gate_col0 = Z_MLA_GATE // (hp * V_DIM)
    return pl.pallas_call(
        functools.partial(_attn_kernel, tq=tq, qc=min(tq, ATTN_CHAIN_COLS), hp=hp),
        grid=(batch, heads // hp, nq),
        in_specs=[
            pl.BlockSpec((1, hp, QK_DIM, tq), lambda b, h, i: (b, h, 0, i)),
            pl.BlockSpec((1, hp, seq, QK_DIM), lambda b, h, i: (b, h, 0, 0)),
            pl.BlockSpec((1, hp, VT_ROWS, seq), lambda b, h, i: (b, h, 0, 0)),
            pl.BlockSpec((tq, hp * V_DIM), lambda b, h, i: (b * nq + i, gate_col0 + h)),
        ],
        out_specs=pl.BlockSpec((tq, hp * V_DIM), lambda b, h, i: (b * nq + i, h)),
        out_shape=jax.ShapeDtypeStruct((batch * seq, heads * V_DIM), BF16),
        scratch_shapes=[pltpu.VMEM((hp, 1, tq), F32), pltpu.VMEM((hp, 1, tq), F32), pltpu.VMEM((hp, VT_ROWS, tq), F32),
                        pltpu.VMEM((hp, tq, tq), BF16)],
        compiler_params=_params(("parallel", "parallel", "arbitrary")),
        name="mla_attention",
    )(qt, k, vt, z)


HALO = 32
SUBLANES = 8
PITCH = 164
PHASES = 8


def _conv_kernel(uval_ref, ugate_ref, cgate_ref, glub_ref, dww_ref, dwb_ref, lng_ref, lnb_ref, wpw_ref, bpw_ref,
                 o_ref, ubuf_ref, cbuf_ref, carry_ref):
    ts, c = uval_ref.shape
    seg = ts // SUBLANES
    n_slabs = c // LANES
    first = HALO - (CONV_K - 1)

    @pl.when(pl.program_id(1) == 0)
    def _():
        carry_ref[...] = jnp.zeros(carry_ref.shape, F32)

    a = uval_ref[...].astype(F32) + glub_ref[:, 0:c]
    g = ugate_ref[...].astype(F32) + glub_ref[:, c:2 * c]
    u = a * _sigmoid(g)
    for s in range(n_slabs):
        us = u[:, s * LANES:(s + 1) * LANES]
        ubuf_ref[s, first:HALO, :] = carry_ref[s, first:HALO, :]
        for k in range(SUBLANES):
            ubuf_ref[s, k * PITCH + HALO:k * PITCH + HALO + seg, :] = us[k * seg:(k + 1) * seg]
            if k + 1 < SUBLANES:
                ubuf_ref[s, (k + 1) * PITCH + first:(k + 1) * PITCH + HALO, :] = (
                    us[(k + 1) * seg - (HALO - first):(k + 1) * seg])
        carry_ref[s, first:HALO, :] = us[ts - (HALO - first):ts]

    def slab_body(s, carry):
        bias = dwb_ref[s]

        def group_body(gi, carry2):
            b0 = gi * PHASES
            accs = [None] * PHASES
            for r in range(CONV_K - 1 + PHASES):
                rows = ubuf_ref[s, pl.ds(b0 + first + r, SUBLANES, stride=PITCH), :]
                for ph in range(PHASES):
                    j = r - ph
                    if 0 <= j < CONV_K:
                        term = dww_ref[s, j:j + 1, :] * rows
                        accs[ph] = term if accs[ph] is None else accs[ph] + term
            for ph in range(PHASES):
                cbuf_ref[s, pl.ds(b0 + ph, SUBLANES, stride=seg), :] = accs[ph] + bias
            return carry2

        return lax.fori_loop(0, seg // PHASES, group_body, carry, unroll=True)

    lax.fori_loop(0, n_slabs, slab_body, 0)

    y = jnp.concatenate([cbuf_ref[s] for s in range(n_slabs)], axis=-1)
    mu = jnp.mean(y, axis=-1, keepdims=True)
    yc = y - mu
    var = jnp.mean(yc * yc, axis=-1, keepdims=True)
    yn = yc * lax.rsqrt(var + EPS) * lng_ref[...] + lnb_ref[...]
    act = _silu(yn).astype(BF16)
    pw = jnp.dot(act, wpw_ref[...], preferred_element_type=F32) + bpw_ref[...]
    o_ref[...] = (pw * _silu(cgate_ref[...].astype(F32))).astype(BF16)


def _conv_module(z, glub, dww, dwb, lng, lnb, wpw, bpw, batch, seq, ts):
    c = wpw.shape[0]
    n_slabs = c // LANES
    tiles_per_seq = seq // ts
    assert ts % (SUBLANES * PHASES) == 0 and HALO + ts // SUBLANES <= PITCH

    def zcol(offset):
        return pl.BlockSpec((ts, c), lambda b, i: (b * tiles_per_seq + i, offset // c))

    def full(arr):
        return pl.BlockSpec(arr.shape, lambda b, i: (0,) * arr.ndim)

    return pl.pallas_call(
        _conv_kernel,
        grid=(batch, tiles_per_seq),
        in_specs=[zcol(Z_CONV_IN), zcol(Z_CONV_IN + c), zcol(Z_CONV_GATE),
                  full(glub), full(dww), full(dwb), full(lng), full(lnb), full(wpw), full(bpw)],
        out_specs=pl.BlockSpec((ts, c), lambda b, i: (b * tiles_per_seq + i, 0)),
        out_shape=jax.ShapeDtypeStruct((batch * seq, c), BF16),
        scratch_shapes=[pltpu.VMEM((n_slabs, SUBLANES * PITCH, LANES), F32), pltpu.VMEM((n_slabs, ts, LANES), F32),
                        pltpu.VMEM((n_slabs, HALO, LANES), F32)],
        compiler_params=_params(("parallel", "arbitrary")),
        name="conv_module",
    )(z, z, z, glub, dww, dwb, lng, lnb, wpw, bpw)


def _slabs(w):
    rows, c = w.shape
    return w.reshape(rows, c // LANES, LANES).transpose(1, 0, 2)


def _outproj_kernel(mla_ref, conv_ref, x_ref, gate_ref, wa_ref, wb_ref, o_ref):
    y = jnp.dot(mla_ref[...], wa_ref[...], preferred_element_type=F32)
    y = y + jnp.dot(conv_ref[...], wb_ref[...], preferred_element_type=F32)
    o_ref[...] = x_ref[...] + gate_ref[0] * y


def _outproj(mla, conv, x2, mod3, w_a, w_b, seq, tm):
    t, d = x2.shape
    half = mla.shape[1]
    tiles_per_seq = seq // tm
    return pl.pallas_call(
        _outproj_kernel,
        grid=(t // tm,),
        in_specs=[
            pl.BlockSpec((tm, half), lambda i: (i, 0)),
            pl.BlockSpec((tm, half), lambda i: (i, 0)),
            pl.BlockSpec((tm, d), lambda i: (i, 0)),
            pl.BlockSpec((1, 1, d), lambda i: (i // tiles_per_seq, 0, 2)),
            pl.BlockSpec((half, d), lambda i: (0, 0), pipeline_mode=pl.Buffered(1)),
            pl.BlockSpec((half, d), lambda i: (0, 0), pipeline_mode=pl.Buffered(1)),
        ],
        out_specs=pl.BlockSpec((tm, d), lambda i: (i, 0)),
        out_shape=jax.ShapeDtypeStruct((t, d), F32),
        compiler_params=_params(("parallel",)),
        name="outproj",
    )(mla, conv, x2, mod3, w_a, w_b)


def _tile(n, want):
    t = min(n, want)
    assert n % t == 0, (n, t)
    return t


def kernel(x, c, positions, ada_w, ada_b, norm_g, w_in, q_lat_g, w_q_up, kv_lat_g, w_kv_up, q_norm_g, k_norm_g,
           glu_b, dw_w, dw_b, conv_ln_g, conv_ln_b, w_pw, b_pw, w_out):
    batch, seq, d = x.shape
    depth = ada_w.shape[0]
    t = batch * seq
    d_mla = N_HEADS * V_DIM
    d_conv = w_pw.shape[1]
    assert w_in.shape[2] == Q_LORA + KV_LORA + ROPE_DIM + d_mla + 3 * d_conv
    assert d_mla == 1024 and d_conv == 1024 and d == 2048

    inv_freq = 1.0 / (ROPE_THETA ** (jnp.arange(0, ROPE_DIM, 2, dtype=F32) / ROPE_DIM))
    pos_t = jnp.broadcast_to(positions.reshape(1, t), (ROPE_DIM // 2, t))
    inv_t = jnp.broadcast_to(inv_freq[:, None], (ROPE_DIM // 2, _tile(t, ROPE_TABLE_COLS)))
    cos_ft, sin_ft = _rope_tables(pos_t, inv_t)

    c_rows = 8
    c_pad = jnp.zeros((c_rows, d), F32).at[:batch].set(c)
    mod = _adaln(c_pad, ada_w, ada_b.reshape(depth, 1, 3 * d), _tile(3 * d, ADALN_COLS))

    x2 = x.reshape(t, d)
    s0 = Q_LORA
    s1 = s0 + KV_LORA
    s2 = s1 + ROPE_DIM
    s3 = s2 + d_mla
    s4 = s3 + 2 * d_conv
    w_z_t = _wprep(jnp.swapaxes(w_in, 1, 2), ((s3, s4), (s2, s3), (s4, w_in.shape[2]), (0, s0), (s0, s1), (s1, s2)),
                   WPREP_COLS)
    for l in range(depth):
        mod3 = mod[l].reshape(c_rows, 1, 3 * d)
        wqt = w_q_up[l].T.astype(BF16)
        wkv = w_kv_up[l].reshape(KV_LORA, N_HEADS, NOPE_DIM + V_DIM)
        wk = wkv[:, :, :NOPE_DIM].reshape(KV_LORA, N_HEADS * NOPE_DIM).astype(BF16)
        wvt = wkv[:, :, NOPE_DIM:].reshape(KV_LORA, N_HEADS * V_DIM).T.astype(BF16)
        gq, gk = q_norm_g[l], k_norm_g[l]
        gkn = gk[None, :NOPE_DIM]
        gkr = jnp.concatenate([gk[NOPE_DIM:], jnp.zeros((LANES - ROPE_DIM,), F32)])[None, :]

        z = _inproj(x2, mod3, norm_g[l][None, :], w_z_t, l, seq, _tile(seq, INPROJ_ROWS), Z_COLS)
        tm_qkv = _tile(seq, QKV_ROWS)
        qt, k, vt = _qkv_up(z, cos_ft, sin_ft, q_lat_g[l][None, :], kv_lat_g[l][None, :], wqt, wk, wvt,
                            jnp.broadcast_to(gq[:, None], (QK_DIM, tm_qkv)), gkn, gkr, batch, seq, tm_qkv)
        mla = _attention(qt, k, vt, z, _tile(seq, ATTN_QUERY_TILE), ATTN_HEADS_PER_STEP)
        dww = jnp.concatenate([dw_w[l], jnp.zeros((HALO - CONV_K, d_conv), F32)], axis=0)
        conv = _conv_module(z, glu_b[l][None, :], _slabs(dww), _slabs(dw_b[l][None, :]), conv_ln_g[l][None, :],
                            conv_ln_b[l][None, :], w_pw[l].astype(BF16), b_pw[l][None, :], batch, seq,
                            _tile(seq, CONV_ROWS))
        wo = w_out[l].astype(BF16)
        x2 = _outproj(mla, conv, x2, mod3, wo[:d_mla], wo[d_mla:], seq, _tile(seq, OUTPROJ_ROWS))
    return x2.reshape(batch, seq, d)
```
